```python
import jax, jax.numpy as jnp
from jax import lax
import numpy as np

D_MODEL = 1024
BATCH = 4
SEQ = 8192
DEPTH = 2

HEAD_DIM = 64
GLA_HEADS = 4
GDN_HEADS = 4
MOBA_HEADS = 8
GLA_W = GLA_HEADS * HEAD_DIM
GDN_W = GDN_HEADS * HEAD_DIM
MOBA_W = MOBA_HEADS * HEAD_DIM
D_MIX = GLA_W + GDN_W + MOBA_W
GLA_GATE_RANK = 16
GLA_GATE_TAU = 16.0
CHUNK = 64
CONV_WIDTH = 4
MOBA_BLOCK = 256
MOBA_TOPK = 3
MOBA_QCHUNK = 32
ROPE_THETA = 10000.0
D_FF = -(-(8 * D_MODEL) // (3 * 256)) * 256
RMS_EPS = 1e-6
IN_SPLITS = (GLA_W, GLA_W, GLA_W, GLA_W, GLA_GATE_RANK,
             GDN_W, GDN_W, GDN_W, GDN_W, GDN_HEADS, GDN_HEADS,
             MOBA_W, MOBA_W, MOBA_W)
N_IN = sum(IN_SPLITS)

kernel_name = "hybrid_gla_gdn_moba_parallel_heads"

F32 = jnp.float32


def rms_norm(x, w):
    xf = x.astype(F32)
    y = xf * lax.rsqrt(jnp.mean(xf * xf, axis=-1, keepdims=True) + RMS_EPS)
    return (y * w.astype(F32)).astype(x.dtype)


def l2_norm(t):
    return t * lax.rsqrt(jnp.sum(t * t, axis=-1, keepdims=True) + RMS_EPS)


def rope(t, pos):
    half = HEAD_DIM // 2
    inv = ROPE_THETA ** (-jnp.arange(half, dtype=F32) / half)
    ang = pos.astype(F32)[:, None] * inv[None, :]
    cos = jnp.cos(ang)[:, None, :]
    sin = jnp.sin(ang)[:, None, :]
    tf = t.astype(F32)
    t1, t2 = tf[..., :half], tf[..., half:]
    return jnp.concatenate([t1 * cos - t2 * sin, t2 * cos + t1 * sin], -1).astype(t.dtype)


def causal_depthwise_conv(x, w):
    c = x.shape[-1]
    return lax.conv_general_dilated(
        x, w.astype(x.dtype)[:, None, :], window_strides=(1,),
        padding=[(CONV_WIDTH - 1, 0)], dimension_numbers=("NWC", "WIO", "NWC"),
        feature_group_count=c)


def gla_chunked(q, k, v, log_a):
    b_, h_, s_, dk = q.shape
    dv = v.shape[-1]
    n = s_ // CHUNK
    to_chunks = lambda t: jnp.moveaxis(t.reshape(b_, h_, n, CHUNK, t.shape[-1]), 2, 0)
    incl = jnp.tril(jnp.ones((CHUNK, CHUNK), bool))

    def step(state, inp):
        qc, kc, vc, gc = inp
        cum = jnp.cumsum(gc, axis=-2)
        o_inter = jnp.einsum('bhcd,bhde->bhce', qc * jnp.exp(cum), state)
        diff = cum[:, :, :, None, :] - cum[:, :, None, :, :]
        decay = jnp.exp(jnp.where(incl[:, :, None], diff, -jnp.inf))
        attn = jnp.einsum('bhid,bhjd,bhijd->bhij', qc, kc, decay)
        o = o_inter + jnp.einsum('bhij,bhje->bhie', attn, vc)
        last = cum[:, :, -1:, :]
        state = jnp.exp(last[:, :, 0, :])[..., None] * state + jnp.einsum(
            'bhcd,bhce->bhde', kc * jnp.exp(last - cum), vc)
        return state, o

    s0 = jnp.zeros((b_, h_, dk, dv), F32)
    _, o = lax.scan(step, s0, (to_chunks(q), to_chunks(k), to_chunks(v), to_chunks(log_a)))
    return jnp.moveaxis(o, 0, 2).reshape(b_, h_, s_, dv)


def gdn_chunked(q, k, v, g, beta):
    b_, h_, s_, dk = q.shape
    dv = v.shape[-1]
    n = s_ // CHUNK
    c = lambda t: t.reshape((b_, h_, n, CHUNK) + t.shape[3:])
    q, k, v, g, beta = c(q), c(k), c(v), c(g), c(beta)
    g = jnp.cumsum(g, axis=-1)
    incl = jnp.tril(jnp.ones((CHUNK, CHUNK), bool))
    strict = jnp.tril(jnp.ones((CHUNK, CHUNK), bool), -1)
    decay = jnp.exp(jnp.where(incl, g[..., :, None] - g[..., None, :], -jnp.inf))
    k_beta = k * beta[..., None]
    v_beta = v * beta[..., None]
    lower = jnp.where(strict, jnp.einsum('bhnid,bhnjd->bhnij', k_beta, k) * decay, 0.0)
    eye = jnp.eye(CHUNK, dtype=F32)
    rhs = jnp.concatenate([v_beta, k_beta * jnp.exp(g)[..., None]], axis=-1)
    sol = lax.linalg.triangular_solve(eye + lower, rhs, left_side=True, lower=True,
                                      unit_diagonal=True)
    u, w = sol[..., :dv], sol[..., dv:]
    attn = jnp.where(incl, jnp.einsum('bhnid,bhnjd->bhnij', q, k) * decay, 0.0)

    def step(state, inp):
        qi, ki, ui, wi, ai, gi = inp
        v_new = ui - jnp.einsum('bhck,bhkv->bhcv', wi, state)
        o = jnp.einsum('bhck,bhkv->bhcv', qi * jnp.exp(gi)[..., None], state) + \
            jnp.einsum('bhij,bhjv->bhiv', ai, v_new)
        g_last = gi[..., -1:]
        state = state * jnp.exp(g_last)[..., None] + jnp.einsum(
            'bhck,bhcv->bhkv', ki * jnp.exp(g_last - gi)[..., None], v_new)
        return state, o

    xs = tuple(jnp.moveaxis(t, 2, 0) for t in (q, k, u, w, attn, g))
    _, o = lax.scan(step, jnp.zeros((b_, h_, dk, dv), F32), xs)
    return jnp.moveaxis(o, 0, 2).reshape(b_, h_, s_, dv)


def gla_mixer(q, k, v, z, a_low, w_gate, b_gate, norm_w):
    b_, s_, _ = q.shape
    heads = lambda t: t.astype(F32).reshape(b_, s_, GLA_HEADS, HEAD_DIM).transpose(0, 2, 1, 3)
    log_a = jax.nn.log_sigmoid((a_low @ w_gate + b_gate).astype(F32)) / GLA_GATE_TAU
    o = gla_chunked(heads(q) * HEAD_DIM ** -0.5, heads(k), heads(v), heads(log_a))
    o = rms_norm(o.transpose(0, 2, 1, 3), norm_w).reshape(b_, s_, GLA_W)
    return (o * jax.nn.silu(z.astype(F32))).astype(q.dtype)


def gdn_mixer(q, k, v, z, b_raw, a_raw, conv_w, a_log, dt_bias, norm_w):
    b_, s_, _ = q.shape
    qkv = jax.nn.silu(causal_depthwise_conv(jnp.concatenate([q, k, v], -1), conv_w))
    q, k, v = jnp.split(qkv.astype(F32), 3, axis=-1)
    heads = lambda t: t.reshape(b_, s_, GDN_HEADS, HEAD_DIM).transpose(0, 2, 1, 3)
    q = l2_norm(heads(q)) * HEAD_DIM ** -0.5
    k = l2_norm(heads(k))
    beta = jax.nn.sigmoid(b_raw.astype(F32)).transpose(0, 2, 1)
    g = (-jnp.exp(a_log.astype(F32)) *
         jax.nn.softplus(a_raw.astype(F32) + dt_bias.astype(F32))).transpose(0, 2, 1)
    o = gdn_chunked(q, k, heads(v), g, beta)
    o = rms_norm(o.transpose(0, 2, 1, 3), norm_w).reshape(b_, s_, GDN_W)
    return (o * jax.nn.silu(z.astype(F32))).astype(z.dtype)


def moba_mixer(q, k, v):
    b_, s_, _ = q.shape
    pos = jnp.arange(s_)
    heads = lambda t: t.reshape(b_, s_, MOBA_HEADS, HEAD_DIM)
    q = rope(heads(q), pos).transpose(0, 2, 1, 3)
    k = rope(heads(k), pos).transpose(0, 2, 1, 3)
    v = heads(v).transpose(0, 2, 1, 3)
    sp = -(-s_ // MOBA_BLOCK) * MOBA_BLOCK
    padw = ((0, 0), (0, 0), (0, sp - s_), (0, 0))
    q, k, v = jnp.pad(q, padw), jnp.pad(k, padw), jnp.pad(v, padw)
    nb = sp // MOBA_BLOCK
    nq = sp // MOBA_QCHUNK
    topk = min(MOBA_TOPK, nb)
    kb = k.reshape(b_, MOBA_HEADS, nb, MOBA_BLOCK, HEAD_DIM)
    vb = v.reshape(b_, MOBA_HEADS, nb, MOBA_BLOCK, HEAD_DIM)
    k_mean = jnp.mean(kb.astype(F32), axis=-2)
    qch = jnp.moveaxis(q.reshape(b_, MOBA_HEADS, nq, MOBA_QCHUNK, HEAD_DIM), 2, 0)
    bi = jnp.arange(b_)[:, None, None, None]
    hi = jnp.arange(MOBA_HEADS)[None, :, None, None]
    scale = HEAD_DIM ** -0.5

    def one_chunk(args):
        qc, ci = args
        q_pos = ci * MOBA_QCHUNK + jnp.arange(MOBA_QCHUNK)
        own = (ci * MOBA_QCHUNK) // MOBA_BLOCK
        gate = jnp.einsum('bhqd,bhnd->bhqn', qc.astype(F32), k_mean)
        gate = jnp.where(jnp.arange(nb) < own, gate, -jnp.inf)
        _, idx = lax.top_k(gate, topk)
        valid = idx < own
        k_sel = kb[bi, hi, idx]
        v_sel = vb[bi, hi, idx]
        s_sel = jnp.einsum('bhqd,bhqnkd->bhqnk', qc, k_sel).astype(F32) * scale
        s_sel = jnp.where(valid[..., None], s_sel, -jnp.inf).reshape(
            b_, MOBA_HEADS, MOBA_QCHUNK, topk * MOBA_BLOCK)
        k_own = lax.dynamic_index_in_dim(kb, own, axis=2, keepdims=False)
        v_own = lax.dynamic_index_in_dim(vb, own, axis=2, keepdims=False)
        s_own = jnp.einsum('bhqd,bhkd->bhqk', qc, k_own).astype(F32) * scale
        key_pos = own * MOBA_BLOCK + jnp.arange(MOBA_BLOCK)
        s_own = jnp.where(key_pos[None, :] <= q_pos[:, None], s_own, -jnp.inf)
        p = jax.nn.softmax(jnp.concatenate([s_sel, s_own], axis=-1), axis=-1)
        p_sel = p[..., :topk * MOBA_BLOCK].reshape(
            b_, MOBA_HEADS, MOBA_QCHUNK, topk, MOBA_BLOCK).astype(v.dtype)
        p_own = p[..., topk * MOBA_BLOCK:].astype(v.dtype)
        return (jnp.einsum('bhqnk,bhqnkd->bhqd', p_sel, v_sel) +
                jnp.einsum('bhqk,bhkd->bhqd', p_own, v_own))

    o = lax.map(one_chunk, (qch, jnp.arange(nq)))
    o = jnp.moveaxis(o, 0, 2).reshape(b_, MOBA_HEADS, sp, HEAD_DIM)[:, :, :s_]
    return o.transpose(0, 2, 1, 3).reshape(b_, s_, MOBA_W)


def setup_inputs(seed: int = 0) -> dict:
    key = jax.random.key(seed)
    ks = jax.random.split(key, 18)
    nrm = lambda kk, shape: jax.random.normal(kk, shape, F32)
    gain = lambda kk, shape: 1.0 + 0.05 * nrm(kk, shape)
    dt = jnp.exp(jax.random.uniform(ks[12], (DEPTH, GDN_HEADS), F32,
                                    np.log(1e-3), np.log(1e-1)))
    return {
        "x": nrm(ks[0], (BATCH, SEQ, D_MODEL)),
        "norm_mix_pre": gain(ks[1], (DEPTH, D_MODEL)),
        "norm_mix_post": gain(ks[2], (DEPTH, D_MODEL)),
        "norm_ffn_pre": gain(ks[3], (DEPTH, D_MODEL)),
        "norm_ffn_post": gain(ks[4], (DEPTH, D_MODEL)),
        "w_in": nrm(ks[5], (DEPTH, D_MODEL, N_IN)) * D_MODEL ** -0.5,
        "w_o": nrm(ks[6], (DEPTH, D_MIX, D_MODEL)) * D_MIX ** -0.5,
        "gla_w_gate": nrm(ks[7], (DEPTH, GLA_GATE_RANK, GLA_W)) * GLA_GATE_RANK ** -0.5,
        "gla_b_gate": 0.1 * nrm(ks[8], (DEPTH, GLA_W)),
        "gla_norm": gain(ks[9], (DEPTH, HEAD_DIM)),
        "gdn_conv": nrm(ks[10], (DEPTH, CONV_WIDTH, 3 * GDN_W)) * CONV_WIDTH ** -0.5,
        "gdn_a_log": jnp.log(jax.random.uniform(ks[11], (DEPTH, GDN_HEADS), F32, 1.0, 16.0)),
        "gdn_dt_bias": dt + jnp.log(-jnp.expm1(-dt)),
        "gdn_norm": gain(ks[13], (DEPTH, HEAD_DIM)),
        "ffn_w_gate": nrm(ks[14], (DEPTH, D_MODEL, D_FF)) * D_MODEL ** -0.5,
        "ffn_w_up": nrm(ks[15], (DEPTH, D_MODEL, D_FF)) * D_MODEL ** -0.5,
        "ffn_w_down": nrm(ks[16], (DEPTH, D_FF, D_MODEL)) * D_FF ** -0.5,
    }


def reference(x, norm_mix_pre, norm_mix_post, norm_ffn_pre, norm_ffn_post, w_in, w_o,
              gla_w_gate, gla_b_gate, gla_norm, gdn_conv, gdn_a_log, gdn_dt_bias, gdn_norm,
              ffn_w_gate, ffn_w_up, ffn_w_down):
    split_points = [int(p) for p in np.cumsum(IN_SPLITS)[:-1]]
    for l in range(DEPTH):
        h = rms_norm(x, norm_mix_pre[l])
        proj = h @ w_in[l]
        (gq, gk, gv, gz, ga, dq, dk, dv, dz, db, da, mq, mk, mv) = jnp.split(
            proj, split_points, axis=-1)
        o_gla = gla_mixer(gq, gk, gv, gz, ga, gla_w_gate[l], gla_b_gate[l], gla_norm[l])
        o_gdn = gdn_mixer(dq, dk, dv, dz, db, da, gdn_conv[l], gdn_a_log[l],
                          gdn_dt_bias[l], gdn_norm[l])
        o_moba = moba_mixer(mq, mk, mv)
        mix = jnp.concatenate([o_gla, o_gdn, o_moba], axis=-1) @ w_o[l]
        x = x + rms_norm(mix, norm_mix_post[l])
        h = rms_norm(x, norm_ffn_pre[l])
        y = (jax.nn.silu(h @ ffn_w_gate[l]) * (h @ ffn_w_up[l])) @ ffn_w_down[l]
        x = x + rms_norm(y, norm_ffn_post[l])
    return x
```

```python
import functools

import numpy as np
import jax
import jax.numpy as jnp
from jax import lax
from jax.experimental import pallas as pl
from jax.experimental.pallas import tpu as pltpu

F32 = jnp.float32
MXU_DTYPE = jnp.bfloat16

HEAD_DIM = 64
GLA_HEADS = 4
GDN_HEADS = 4
MOBA_HEADS = 8
GLA_W = GLA_HEADS * HEAD_DIM
GDN_W = GDN_HEADS * HEAD_DIM
MOBA_W = MOBA_HEADS * HEAD_DIM
GLA_GATE_RANK = 16
GLA_GATE_TAU = 16.0
CHUNK = 64
CHUNK_LEVELS = 6
CONV_WIDTH = 4
MOBA_BLOCK = 256
MOBA_TOPK = 3
ROPE_THETA = 10000.0
RMS_EPS = 1e-6
LANES = 128
NEG_BIG = -1e30
VMEM_LIMIT = 48 * 1024 * 1024


def _mm(a, b):
    return jnp.dot(a.astype(MXU_DTYPE), b.astype(MXU_DTYPE), preferred_element_type=F32)


def _mm_nt(a, b):
    return lax.dot_general(a.astype(MXU_DTYPE), b.astype(MXU_DTYPE),
                           (((1,), (1,)), ((), ())), preferred_element_type=F32)


def _mm_tn(a, b):
    return lax.dot_general(a.astype(MXU_DTYPE), b.astype(MXU_DTYPE),
                           (((0,), (0,)), ((), ())), preferred_element_type=F32)


def _split2(x):
    hi = x.astype(MXU_DTYPE)
    lo = (x - hi.astype(F32)).astype(MXU_DTYPE)
    return hi, lo


def _split3(x):
    hi = x.astype(MXU_DTYPE)
    r = x - hi.astype(F32)
    mid = r.astype(MXU_DTYPE)
    lo = (r - mid.astype(F32)).astype(MXU_DTYPE)
    return hi, mid, lo


def _mm_exact_rhs(c, x):
    hi, lo = _split2(x)
    return _mm(c, hi) + _mm(c, lo)


def _mm_exact_lhs3(x, c):
    hi, mid, lo = _split3(x)
    return _mm(hi, c) + _mm(mid, c) + _mm(lo, c)


def _sigmoid(x):
    return 1.0 / (1.0 + jnp.exp(-x))


def _softplus(x):
    return jnp.maximum(x, 0.0) + jnp.log(1.0 + jnp.exp(-jnp.abs(x)))


def _silu(x):
    return x * _sigmoid(x)


def _tile_rows(x, n):
    return jnp.concatenate([x] * n, axis=0)


def _block_diag_mask(rows, cols, rblk, cblk):
    r = lax.broadcasted_iota(jnp.int32, (rows, cols), 0) // rblk
    c = lax.broadcasted_iota(jnp.int32, (rows, cols), 1) // cblk
    return r == c


def _params(sem):
    return pltpu.CompilerParams(dimension_semantics=sem, vmem_limit_bytes=VMEM_LIMIT)


def _inproj_kernel(x_ref, nw_ref, w1_ref, w2_ref, w3_ref, o1_ref, o2_ref, o3_ref):
    x = x_ref[...]
    ms = jnp.mean(x * x, axis=-1, keepdims=True)
    h = (x * lax.rsqrt(ms + RMS_EPS) * nw_ref[...]).astype(MXU_DTYPE)
    o1_ref[...] = jnp.dot(h, w1_ref[...], preferred_element_type=F32)
    o2_ref[...] = jnp.dot(h, w2_ref[...], preferred_element_type=F32)
    o3_ref[...] = jnp.dot(h, w3_ref[...], preferred_element_type=F32)


def _inproj(xf, nw, w1, w2, w3, tm):
    t, d = xf.shape
    n1, n2, n3 = w1.shape[1], w2.shape[1], w3.shape[1]
    row = lambda i: (i, 0)
    fixed = lambda i: (0, 0)
    return pl.pallas_call(
        _inproj_kernel,
        grid=(t // tm,),
        in_specs=[pl.BlockSpec((tm, d), row), pl.BlockSpec((1, d), fixed),
                  pl.BlockSpec((d, n1), fixed), pl.BlockSpec((d, n2), fixed),
                  pl.BlockSpec((d, n3), fixed)],
        out_specs=[pl.BlockSpec((tm, n1), row), pl.BlockSpec((tm, n2), row),
                   pl.BlockSpec((tm, n3), row)],
        out_shape=[jax.ShapeDtypeStruct((t, n1), F32), jax.ShapeDtypeStruct((t, n2), F32),
                   jax.ShapeDtypeStruct((t, n3), F32)],
        compiler_params=_params(("arbitrary",)),
        name="inproj",
    )(xf, nw, w1, w2, w3)


def _gla_level_matrix():
    i = np.arange(CHUNK)[:, None]
    t = np.arange(CHUNK)[None, :]
    mats = [(t <= i)]
    for lvl in range(CHUNK_LEVELS):
        r = ((i >> (lvl + 1)) << (lvl + 1)) + (1 << lvl) - 1
        mats.append((t > np.minimum(i, r)) & (t <= np.maximum(i, r)))
    return np.concatenate(mats, axis=0).astype(np.float32)


def _gla_kernel(q_ref, k_ref, v_ref, z_ref, a_ref, wg_ref, bg_ref, nw_ref, lvl_ref,
                o_ref, st_ref, *, rows):
    @pl.when(pl.program_id(2) == 0)
    def _():
        st_ref[...] = jnp.zeros_like(st_ref)

    pre = _mm(a_ref[...], wg_ref[...]) + bg_ref[...]
    log_a = -_softplus(-pre) * (1.0 / GLA_GATE_TAU)

    lane = lax.broadcasted_iota(jnp.int32, (CHUNK, LANES), 1)
    head0 = lane < HEAD_DIM
    ridx = lax.broadcasted_iota(jnp.int32, (CHUNK, LANES), 0)
    head_ones = _block_diag_mask(LANES, LANES, HEAD_DIM, HEAD_DIM)
    head_ones_f = head_ones.astype(MXU_DTYPE)
    ri2 = lax.broadcasted_iota(jnp.int32, (2 * CHUNK, CHUNK), 0) % CHUNK
    ci2 = lax.broadcasted_iota(jnp.int32, (2 * CHUNK, CHUNK), 1)
    same_blk = [(ri2 >> (l + 1)) == (ci2 >> (l + 1)) for l in range(CHUNK_LEVELS)]
    upper = [((ridx >> l) & 1) == 1 for l in range(CHUNK_LEVELS)]
    lvl = lvl_ref[...]

    st = st_ref[...]
    outs = []
    for c in range(rows // CHUNK):
        sl = slice(c * CHUNK, (c + 1) * CHUNK)
        q = q_ref[sl, :] * (HEAD_DIM ** -0.5)
        k = k_ref[sl, :]
        v = v_ref[sl, :]
        dall = _mm_exact_rhs(lvl, log_a[sl, :])
        eall = jnp.exp(dall)
        cum = dall[0:CHUNK]
        last = cum[CHUNK - 1:CHUNK, :]
        q_dec = q * eall[0:CHUNK]
        k_dec = k * jnp.exp(last - cum)
        e_last = eall[CHUNK - 1:CHUNK, :]

        a_st = jnp.zeros((2 * CHUNK, CHUNK), F32)
        for l in range(CHUNK_LEVELS):
            e = eall[(l + 1) * CHUNK:(l + 2) * CHUNK]
            ql = jnp.where(upper[l], q * e, 0.0)
            kl = jnp.where(upper[l], 0.0, k * e)
            lhs = jnp.concatenate([jnp.where(head0, ql, 0.0), jnp.where(head0, 0.0, ql)], axis=0)
            a_st = a_st + jnp.where(same_blk[l], _mm_nt(lhs, kl), 0.0)
        o_full = _mm(a_st, v)
        o_intra = jnp.where(head0, o_full[0:CHUNK], o_full[CHUNK:2 * CHUNK])
        diag = _mm(q * k, head_ones_f)
        outs.append(o_intra + diag * v + _mm_nt(q_dec, st))
        st = st * e_last + jnp.where(head_ones, _mm_tn(v, k_dec), 0.0)
    st_ref[...] = st

    o_all = jnp.concatenate(outs, axis=0)
    ms = _mm(o_all * o_all, head_ones_f) * (1.0 / HEAD_DIM)
    y = o_all * lax.rsqrt(ms + RMS_EPS) * nw_ref[...]
    o_ref[...] = y * _silu(z_ref[...])


def _gla(pg, wg, bg, nw, lvl, batch, seq, rows):
    t = pg.shape[0]
    nt = seq // rows
    pairs = GLA_W // LANES
    col = lambda off: (lambda b, p, i: (b * nt + i, off + p))
    fixed = lambda b, p, i: (0, 0)
    blk = lambda off: pl.BlockSpec((rows, LANES), col(off))
    return pl.pallas_call(
        functools.partial(_gla_kernel, rows=rows),
        grid=(batch, pairs, nt),
        in_specs=[blk(0), blk(pairs), blk(2 * pairs), blk(3 * pairs),
                  pl.BlockSpec((rows, LANES), lambda b, p, i: (b * nt + i, 4 * pairs)),
                  pl.BlockSpec((LANES, LANES), lambda b, p, i: (0, p)),
                  pl.BlockSpec((1, LANES), lambda b, p, i: (0, p)),
                  pl.BlockSpec((1, LANES), fixed),
                  pl.BlockSpec(lvl.shape, fixed)],
        out_specs=pl.BlockSpec((rows, LANES), lambda b, p, i: (b * nt + i, p)),
        out_shape=jax.ShapeDtypeStruct((t, GLA_W), F32),
        scratch_shapes=[pltpu.VMEM((LANES, LANES), F32)],
        compiler_params=_params(("arbitrary", "arbitrary", "arbitrary")),
        name="gla",
    )(pg, pg, pg, pg, pg, wg, bg, nw, lvl)


def _gdn_kernel(q_ref, k_ref, v_ref, z_ref, e_ref, cq_ref, ck_ref, cv_ref, alog_ref, dt_ref,
                nw_ref, tri_ref, indb_ref, indg_ref, o_ref, bq_ref, bk_ref, bv_ref, st_ref,
                *, rows):
    w = GDN_W
    halo = 8

    @pl.when(pl.program_id(1) == 0)
    def _():
        st_ref[...] = jnp.zeros_like(st_ref)
        for buf in (bq_ref, bk_ref, bv_ref):
            buf[0:halo, :] = jnp.zeros((halo, w), F32)

    def conv_silu(x_ref, buf, cw_ref):
        buf[halo:halo + rows, :] = x_ref[...]
        acc = jnp.zeros((rows, w), F32)
        for i in range(CONV_WIDTH):
            acc = acc + cw_ref[i:i + 1, :] * buf[pl.ds(halo - (CONV_WIDTH - 1) + i, rows), :]
        buf[0:halo, :] = buf[rows:rows + halo, :]
        return _silu(acc)

    q = conv_silu(q_ref, bq_ref, cq_ref)
    k = conv_silu(k_ref, bk_ref, ck_ref)
    v = conv_silu(v_ref, bv_ref, cv_ref)

    head_ones = _block_diag_mask(w, w, HEAD_DIM, HEAD_DIM)
    head_ones_f = head_ones.astype(MXU_DTYPE)

    def sumsq(x):
        hi, lo = _split2(x * x)
        return _mm(hi, head_ones_f) + _mm(lo, head_ones_f)

    q = q * lax.rsqrt(sumsq(q) + RMS_EPS) * (HEAD_DIM ** -0.5)
    k = k * lax.rsqrt(sumsq(k) + RMS_EPS)

    extra = e_ref[...]
    beta_e = _mm_exact_lhs3(_sigmoid(extra), indb_ref[...])
    g_small = -jnp.exp(alog_ref[...]) * _softplus(extra + dt_ref[...])

    lane = lax.broadcasted_iota(jnp.int32, (CHUNK, w), 1) % HEAD_DIM
    ridx = lax.broadcasted_iota(jnp.int32, (CHUNK, w), 0)
    incl = lane <= ridx
    strict = lane < ridx
    r2 = lax.broadcasted_iota(jnp.int32, (w, w), 0)
    c2 = lax.broadcasted_iota(jnp.int32, (w, w), 1)
    eye = (r2 == c2).astype(F32)
    tri = tri_ref[...]
    indg = indg_ref[...]

    st = st_ref[...]
    outs = []
    for c in range(rows // CHUNK):
        sl = slice(c * CHUNK, (c + 1) * CHUNK)
        qc, kc, vc, bc = q[sl], k[sl], v[sl], beta_e[sl]
        h3 = _split3(g_small[sl])
        gc_small = _mm(tri, h3[0]) + _mm(tri, h3[1]) + _mm(tri, h3[2])
        gc = _mm_exact_lhs3(gc_small, indg)
        eg = jnp.exp(gc)
        g_last = gc[CHUNK - 1:CHUNK, :]
        kb = kc * bc
        vb = vc * bc
        kbg = kb * eg
        qg = qc * eg
        k_dec = kc * jnp.exp(g_last - gc)
        e_last = jnp.exp(g_last)

        g_hi, g_mid, g_lo = [p.astype(F32) for p in _split3(gc)]
        one = jnp.ones_like(gc)
        zero = jnp.zeros_like(gc)
        a2 = jnp.where(lane == 0, g_hi, jnp.where(lane == 1, g_mid, jnp.where(
            lane == 2, g_lo, jnp.where(lane < 6, one, zero))))
        b2 = jnp.where(lane < 3, one, jnp.where(lane == 3, -g_hi, jnp.where(
            lane == 4, -g_mid, jnp.where(lane == 5, -g_lo, zero))))
        diff = _mm_nt(a2, jnp.where(head_ones, _tile_rows(b2, GDN_HEADS), 0.0))
        decay = jnp.exp(jnp.where(incl, diff, NEG_BIG))

        k_bd = jnp.where(head_ones, _tile_rows(kc, GDN_HEADS), 0.0)
        kk = _mm_nt(jnp.concatenate([kb, qc], axis=0), k_bd)
        lower = jnp.where(strict, kk[0:CHUNK] * decay, 0.0)
        attn = kk[CHUNK:2 * CHUNK] * decay

        xk = jnp.where(head_ones, _tile_rows(-lower, GDN_HEADS), 0.0)
        p = eye + xk
        for _ in range(CHUNK_LEVELS - 1):
            xk = _mm(xk, xk)
            p = p + _mm(xk, p)
        t_all = p[0:CHUNK]
        for h in range(1, GDN_HEADS):
            t_all = t_all + p[h * CHUNK:(h + 1) * CHUNK]

        rhs = jnp.concatenate([jnp.where(head_ones, _tile_rows(vb, GDN_HEADS), 0.0),
                               jnp.where(head_ones, _tile_rows(kbg, GDN_HEADS), 0.0)], axis=1)
        uw = _mm(t_all, rhs)
        u, wmat = uw[:, 0:w], uw[:, w:2 * w]

        r = _mm_nt(jnp.concatenate([wmat, qg], axis=0), st)
        v_new = u - r[0:CHUNK]
        vn_bd = jnp.where(head_ones, _tile_rows(v_new, GDN_HEADS), 0.0)
        outs.append(r[CHUNK:2 * CHUNK] + _mm(attn, vn_bd))
        st = st * e_last + jnp.where(head_ones, _mm_tn(v_new, k_dec), 0.0)
    st_ref[...] = st

    o_all = jnp.concatenate(outs, axis=0)
    ms = _mm(o_all * o_all, head_ones_f) * (1.0 / HEAD_DIM)
    y = o_all * lax.rsqrt(ms + RMS_EPS) * nw_ref[...]
    o_ref[...] = y * _silu(z_ref[...])


def _gdn(pd, cq, ck, cv, alog, dt, nw, tri, indb, indg, batch, seq, rows):
    t = pd.shape[0]
    nt = seq // rows
    w = GDN_W
    fixed = lambda b, i: (0, 0)
    blk = lambda off: pl.BlockSpec((rows, w), lambda b, i: (b * nt + i, off))
    small = lambda a: pl.BlockSpec(a.shape, fixed)
    return pl.pallas_call(
        functools.partial(_gdn_kernel, rows=rows),
        grid=(batch, nt),
        in_specs=[blk(0), blk(1), blk(2), blk(3),
                  pl.BlockSpec((rows, LANES), lambda b, i: (b * nt + i, 4 * w // LANES)),
                  small(cq), small(ck), small(cv), small(alog), small(dt), small(nw),
                  small(tri), small(indb), small(indg)],
        out_specs=pl.BlockSpec((rows, w), lambda b, i: (b * nt + i, 0)),
        out_shape=jax.ShapeDtypeStruct((t, w), F32),
        scratch_shapes=[pltpu.VMEM((rows + 8, w), F32), pltpu.VMEM((rows + 8, w), F32),
                        pltpu.VMEM((rows + 8, w), F32), pltpu.VMEM((w, w), F32)],
        compiler_params=_params(("arbitrary", "arbitrary")),
        name="gdn",
    )(pd, pd, pd, pd, pd, cq, ck, cv, alog, dt, nw, tri, indb, indg)


def _moba_prep_kernel(q_ref, k_ref, v_ref, cos_ref, sin_ref, p1_ref, p2_ref,
                      qa_ref, ka_ref, vb_ref, km_ref, *, nblk):
    n = pl.program_id(1)
    blk = MOBA_BLOCK
    hw = MOBA_W

    @pl.when(n == 0)
    def _():
        km_ref[...] = jnp.zeros_like(km_ref)

    cosf = cos_ref[...]
    sinf = sin_ref[...]
    lane = lax.broadcasted_iota(jnp.int32, (blk, LANES), 1)
    first_half = (lane % HEAD_DIM) < (HEAD_DIM // 2)

    def rope(x_ref):
        parts = []
        for c in range(hw // LANES):
            x = x_ref[:, c * LANES:(c + 1) * LANES]
            swapped = jnp.where(first_half, pltpu.roll(x, LANES - HEAD_DIM // 2, 1),
                                pltpu.roll(x, HEAD_DIM // 2, 1))
            parts.append(x * cosf + swapped * sinf)
        return jnp.concatenate(parts, axis=1)

    q = rope(q_ref) * (HEAD_DIM ** -0.5)
    k = rope(k_ref)

    km_rows = jnp.where(_block_diag_mask(MOBA_HEADS * nblk, hw, nblk, HEAD_DIM),
                        _tile_rows(km_ref[...], MOBA_HEADS), 0.0)
    gate = lax.dot_general(km_rows, q, (((1,), (1,)), ((), ())),
                           precision=lax.Precision.HIGHEST, preferred_element_type=F32)
    gate = gate.reshape(MOBA_HEADS, nblk, blk)
    kb = lax.broadcasted_iota(jnp.int32, (MOBA_HEADS, nblk, blk), 1)
    past = kb < n
    g = jnp.where(past, gate, -jnp.inf)
    sel = kb == n
    for _ in range(MOBA_TOPK):
        m = jnp.max(g, axis=1, keepdims=True)
        first = jnp.min(jnp.where(g == m, kb, nblk), axis=1, keepdims=True)
        pick = kb == first
        sel = sel | (pick & past)
        g = jnp.where(pick, -jnp.inf, g)
    bias_t = jnp.where(sel, 0.0, NEG_BIG).reshape(MOBA_HEADS * nblk, blk)
    bias = bias_t.T

    qa_ref[...] = (_mm(q, p1_ref[...]) + _mm(bias, p2_ref[...])).astype(qa_ref.dtype)
    lane_a = lax.broadcasted_iota(jnp.int32, (blk, MOBA_HEADS * LANES), 1) % LANES
    onehot = (lane_a == HEAD_DIM + n).astype(F32)
    ka_ref[...] = (_mm(k, p1_ref[...]) + onehot).astype(ka_ref.dtype)
    vb_ref[...] = v_ref[...].astype(vb_ref.dtype)
    km_ref[pl.ds(n, 1), :] = jnp.mean(k, axis=0, keepdims=True)


def _moba_prep(pm, cos, sin, p1, p2, batch, seq):
    t = pm.shape[0]
    nblk = seq // MOBA_BLOCK
    hw = MOBA_W
    aw = MOBA_HEADS * LANES
    fixed = lambda b, i: (0, 0)
    blk = lambda off: pl.BlockSpec((MOBA_BLOCK, hw), lambda b, i: (b * nblk + i, off))
    tab = pl.BlockSpec((MOBA_BLOCK, LANES), lambda b, i: (i, 0))
    row_out = lambda width: pl.BlockSpec((MOBA_BLOCK, width), lambda b, i: (b * nblk + i, 0))
    return pl.pallas_call(
        functools.partial(_moba_prep_kernel, nblk=nblk),
        grid=(batch, nblk),
        in_specs=[blk(0), blk(1), blk(2), tab, tab,
                  pl.BlockSpec(p1.shape, fixed), pl.BlockSpec(p2.shape, fixed)],
        out_specs=[row_out(aw), row_out(aw), row_out(hw)],
        out_shape=[jax.ShapeDtypeStruct((t, aw), MXU_DTYPE),
                   jax.ShapeDtypeStruct((t, aw), MXU_DTYPE),
                   jax.ShapeDtypeStruct((t, hw), MXU_DTYPE)],
        scratch_shapes=[pltpu.VMEM((nblk, hw), F32)],
        compiler_params=_params(("arbitrary", "arbitrary")),
        name="moba_prep",
    )(pm, pm, pm, cos, sin, p1, p2)


def _moba_attn_kernel(qa_ref, ka_ref, v_ref, o_ref):
    n = pl.program_id(2)
    blk = MOBA_BLOCK
    row = lax.broadcasted_iota(jnp.int32, (blk, blk), 0)
    col = lax.broadcasted_iota(jnp.int32, (blk, blk), 1)
    causal = col <= row
    qs = [qa_ref[:, h * LANES:(h + 1) * LANES] for h in range(2)]

    def block_scores(j):
        off = pl.multiple_of(j * blk, blk)
        vj = v_ref[pl.ds(off, blk), :]
        return [_mm_nt(qs[h], ka_ref[pl.ds(off, blk), h * LANES:(h + 1) * LANES])
                for h in range(2)], vj

    s_own, v_own = block_scores(n)
    carry = []
    for h in range(2):
        s = jnp.where(causal, s_own[h], NEG_BIG)
        m = jnp.max(s, axis=-1, keepdims=True)
        p = jnp.exp(s - m)
        carry += [m, jnp.sum(p, axis=-1, keepdims=True), _mm(p, v_own)]

    def body(j, carry):
        s_j, vj = block_scores(j)
        new = []
        for h in range(2):
            m, l, acc = carry[3 * h:3 * h + 3]
            m_new = jnp.maximum(m, jnp.max(s_j[h], axis=-1, keepdims=True))
            alpha = jnp.exp(m - m_new)
            p = jnp.exp(s_j[h] - m_new)
            new += [m_new, alpha * l + jnp.sum(p, axis=-1, keepdims=True),
                    alpha * acc + _mm(p, vj)]
        return tuple(new)

    carry = lax.fori_loop(0, n, body, tuple(carry))
    lane = lax.broadcasted_iota(jnp.int32, (blk, LANES), 1)
    o_ref[...] = jnp.where(lane < HEAD_DIM, carry[2] / carry[1], carry[5] / carry[4])


def _moba_attn(qa, ka, vb, batch, seq):
    t = qa.shape[0]
    nblk = seq // MOBA_BLOCK
    pairs = MOBA_HEADS // 2
    return pl.pallas_call(
        _moba_attn_kernel,
        grid=(batch, pairs, nblk),
        in_specs=[pl.BlockSpec((MOBA_BLOCK, 2 * LANES), lambda b, p, i: (b * nblk + i, p)),
                  pl.BlockSpec((seq, 2 * LANES), lambda b, p, i: (b, p)),
                  pl.BlockSpec((seq, LANES), lambda b, p, i: (b, p))],
        out_specs=pl.BlockSpec((MOBA_BLOCK, LANES), lambda b, p, i: (b * nblk + i, p)),
        out_shape=jax.ShapeDtypeStruct((t, MOBA_W), F32),
        compiler_params=_params(("arbitrary", "arbitrary", "arbitrary")),
        name="moba_attn",
    )(qa, ka, vb)


def _outproj_kernel(og_ref, od_ref, om_ref, x_ref, w1_ref, w2_ref, w3_ref, nw_ref, o_ref):
    mix = (_mm(og_ref[...], w1_ref[...]) + _mm(od_ref[...], w2_ref[...])
           + _mm(om_ref[...], w3_ref[...]))
    ms = jnp.mean(mix * mix, axis=-1, keepdims=True)
    o_ref[...] = x_ref[...] + mix * lax.rsqrt(ms + RMS_EPS) * nw_ref[...]


def _outproj(og, od, om, xf, w1, w2, w3, nw, tm):
    t, d = xf.shape
    row = lambda i: (i, 0)
    fixed = lambda i: (0, 0)
    full = lambda a: pl.BlockSpec(a.shape, fixed)
    tile = lambda a: pl.BlockSpec((tm, a.shape[1]), row)
    return pl.pallas_call(
        _outproj_kernel,
        grid=(t // tm,),
        in_specs=[tile(og), tile(od), tile(om), tile(xf), full(w1), full(w2), full(w3), full(nw)],
        out_specs=pl.BlockSpec((tm, d), row),
        out_shape=jax.ShapeDtypeStruct((t, d), F32),
        compiler_params=_params(("arbitrary",)),
        name="outproj",
    )(og, od, om, xf, w1, w2, w3, nw)


def _ffn_kernel(x_ref, npre_ref, wg_ref, wu_ref, wd_ref, npost_ref, o_ref, h_ref, acc_ref):
    f = pl.program_id(1)

    @pl.when(f == 0)
    def _():
        x = x_ref[...]
        ms = jnp.mean(x * x, axis=-1, keepdims=True)
        h_ref[...] = (x * lax.rsqrt(ms + RMS_EPS) * npre_ref[...]).astype(h_ref.dtype)
        acc_ref[...] = jnp.zeros_like(acc_ref)

    h = h_ref[...]
    g = jnp.dot(h, wg_ref[...], preferred_element_type=F32)
    u = jnp.dot(h, wu_ref[...], preferred_element_type=F32)
    acc_ref[...] += _mm(_silu(g) * u, wd_ref[...])

    @pl.when(f == pl.num_programs(1) - 1)
    def _():
        y = acc_ref[...]
        ms = jnp.mean(y * y, axis=-1, keepdims=True)
        o_ref[...] = x_ref[...] + y * lax.rsqrt(ms + RMS_EPS) * npost_ref[...]


def _ffn(xf, npre, wg, wu, wd, npost, tm, tf):
    t, d = xf.shape
    dff = wg.shape[1]
    return pl.pallas_call(
        _ffn_kernel,
        grid=(t // tm, dff // tf),
        in_specs=[pl.BlockSpec((tm, d), lambda i, f: (i, 0)),
                  pl.BlockSpec((1, d), lambda i, f: (0, 0)),
                  pl.BlockSpec((d, tf), lambda i, f: (0, f)),
                  pl.BlockSpec((d, tf), lambda i, f: (0, f)),
                  pl.BlockSpec((tf, d), lambda i, f: (f, 0)),
                  pl.BlockSpec((1, d), lambda i, f: (0, 0))],
        out_specs=pl.BlockSpec((tm, d), lambda i, f: (i, 0)),
        out_shape=jax.ShapeDtypeStruct((t, d), F32),
        scratch_shapes=[pltpu.VMEM((tm, d), MXU_DTYPE), pltpu.VMEM((tm, d), F32)],
        compiler_params=_params(("arbitrary", "arbitrary")),
        name="ffn",
    )(xf, npre, wg, wu, wd, npost)


def _tile_sizes(tokens, seq):
    pick = lambda want, n: want if n % want == 0 else MOBA_BLOCK
    return dict(proj=pick(512, tokens), ffn=pick(1024, tokens), rec=pick(256, seq), ffn_cols=256)


def _rope_tables(seq):
    half = HEAD_DIM // 2
    inv = ROPE_THETA ** (-jnp.arange(half, dtype=F32) / half)
    ang = jnp.arange(seq).astype(F32)[:, None] * inv[None, :]
    cos, sin = jnp.cos(ang), jnp.sin(ang)
    reps = LANES // HEAD_DIM
    return (jnp.tile(jnp.concatenate([cos, cos], axis=1), (1, reps)),
            jnp.tile(jnp.concatenate([-sin, sin], axis=1), (1, reps)))


def _moba_placement(nblk):
    aw = MOBA_HEADS * LANES
    p1 = np.zeros((MOBA_W, aw), np.float32)
    p2 = np.zeros((MOBA_HEADS * nblk, aw), np.float32)
    for h in range(MOBA_HEADS):
        for d in range(HEAD_DIM):
            p1[h * HEAD_DIM + d, h * LANES + d] = 1.0
        for n in range(nblk):
            p2[h * nblk + n, h * LANES + HEAD_DIM + n] = 1.0
    return jnp.asarray(p1, MXU_DTYPE), jnp.asarray(p2, MXU_DTYPE)


def _head_indicator(first_lane, heads):
    ind = np.zeros((LANES, heads * HEAD_DIM), np.float32)
    for h in range(heads):
        ind[first_lane + h, h * HEAD_DIM:(h + 1) * HEAD_DIM] = 1.0
    return jnp.asarray(ind, MXU_DTYPE)


def _lane_row(values, first_lane):
    return jnp.zeros((1, LANES), F32).at[0, first_lane:first_lane + values.shape[0]].set(
        values.astype(F32))


def kernel(x, norm_mix_pre, norm_mix_post, norm_ffn_pre, norm_ffn_post, w_in, w_o, gla_w_gate,
           gla_b_gate, gla_norm, gdn_conv, gdn_a_log, gdn_dt_bias, gdn_norm, ffn_w_gate, ffn_w_up,
           ffn_w_down):
    batch, seq, d_model = x.shape
    depth = w_in.shape[0]
    tokens = batch * seq
    nblk = seq // MOBA_BLOCK
    assert seq % MOBA_BLOCK == 0 and nblk + HEAD_DIM <= LANES
    ts = _tile_sizes(tokens, seq)

    cos, sin = _rope_tables(seq)
    p1, p2 = _moba_placement(nblk)
    lvl = jnp.asarray(_gla_level_matrix(), MXU_DTYPE)
    tri = jnp.asarray(np.tril(np.ones((CHUNK, CHUNK), np.float32)), MXU_DTYPE)
    indb = _head_indicator(0, GDN_HEADS)
    indg = _head_indicator(GDN_HEADS, GDN_HEADS)
    row2d = lambda a: a.reshape(1, -1).astype(F32)
    pad_cols = lambda a, n: jnp.pad(a, ((0, 0), (0, n - a.shape[1])))
    bf = lambda a: a.astype(MXU_DTYPE)

    o_ga = 4 * GLA_W
    o_d = o_ga + GLA_GATE_RANK
    o_db = o_d + 4 * GDN_W
    o_m = o_db + 2 * GDN_HEADS

    xf = x.reshape(tokens, d_model)
    for l in range(depth):
        wl = w_in[l]
        w_gla = bf(jnp.concatenate([wl[:, :o_ga], pad_cols(wl[:, o_ga:o_d], LANES)], axis=1))
        w_gdn = bf(jnp.concatenate([wl[:, o_d:o_db], pad_cols(wl[:, o_db:o_m], LANES)], axis=1))
        w_moba = bf(wl[:, o_m:])
        pg, pd, pm = _inproj(xf, row2d(norm_mix_pre[l]), w_gla, w_gdn, w_moba, ts["proj"])

        wgate = bf(jnp.pad(gla_w_gate[l], ((0, LANES - GLA_GATE_RANK), (0, 0))))
        o_gla = _gla(pg, wgate, row2d(gla_b_gate[l]),
                     row2d(jnp.tile(gla_norm[l], LANES // HEAD_DIM)), lvl, batch, seq, ts["rec"])

        conv = gdn_conv[l].astype(F32)
        o_gdn = _gdn(pd, conv[:, :GDN_W], conv[:, GDN_W:2 * GDN_W], conv[:, 2 * GDN_W:],
                     _lane_row(gdn_a_log[l], GDN_HEADS), _lane_row(gdn_dt_bias[l], GDN_HEADS),
                     row2d(jnp.tile(gdn_norm[l], GDN_HEADS)), tri, indb, indg,
                     batch, seq, ts["rec"])

        qa, ka, vb = _moba_prep(pm, cos, sin, p1, p2, batch, seq)
        o_moba = _moba_attn(qa, ka, vb, batch, seq)

        wo = bf(w_o[l])
        xf = _outproj(o_gla, o_gdn, o_moba, xf, wo[:GLA_W], wo[GLA_W:GLA_W + GDN_W],
                      wo[GLA_W + GDN_W:], row2d(norm_mix_post[l]), ts["proj"])
        xf = _ffn(xf, row2d(norm_ffn_pre[l]), bf(ffn_w_gate[l]), bf(ffn_w_up[l]),
                  bf(ffn_w_down[l]), row2d(norm_ffn_post[l]), ts["ffn"], ts["ffn_cols"])
    return xf.reshape(batch, seq, d_model)
```

```python
import functools

import numpy as np
import jax
import jax.numpy as jnp
from jax import lax
from jax.experimental import pallas as pl
from jax.experimental.pallas import tpu as pltpu

F32 = jnp.float32
MXU_DTYPE = jnp.bfloat16

HEAD_DIM = 64
GLA_HEADS = 4
GDN_HEADS = 4
MOBA_HEADS = 8
GLA_W = GLA_HEADS * HEAD_DIM
GDN_W = GDN_HEADS * HEAD_DIM
MOBA_W = MOBA_HEADS * HEAD_DIM
GLA_GATE_RANK = 16
GLA_GATE_TAU = 16.0
CHUNK = 64
CHUNK_LEVELS = 6
CONV_WIDTH = 4
MOBA_BLOCK = 256
MOBA_TOPK = 3
ROPE_THETA = 10000.0
RMS_EPS = 1e-6
LANES = 128
NEG_BIG = -1e30
LOG2_E = 1.4426950408889634
VMEM_LIMIT = 48 * 1024 * 1024


def _mm(a, b):
    return jnp.dot(a.astype(MXU_DTYPE), b.astype(MXU_DTYPE), preferred_element_type=F32)


def _mm_nt(a, b):
    return lax.dot_general(a.astype(MXU_DTYPE), b.astype(MXU_DTYPE),
                           (((1,), (1,)), ((), ())), preferred_element_type=F32)


def _mm_tn(a, b):
    return lax.dot_general(a.astype(MXU_DTYPE), b.astype(MXU_DTYPE),
                           (((0,), (0,)), ((), ())), preferred_element_type=F32)


def _split2(x):
    hi = x.astype(MXU_DTYPE)
    lo = (x - hi.astype(F32)).astype(MXU_DTYPE)
    return hi, lo


def _split3(x):
    hi = x.astype(MXU_DTYPE)
    r = x - hi.astype(F32)
    mid = r.astype(MXU_DTYPE)
    lo = (r - mid.astype(F32)).astype(MXU_DTYPE)
    return hi, mid, lo


def _mm_exact_rhs(c, x):
    hi, lo = _split2(x)
    return _mm(c, hi) + _mm(c, lo)


def _mm_exact_lhs3(x, c):
    hi, mid, lo = _split3(x)
    return _mm(hi, c) + _mm(mid, c) + _mm(lo, c)


def _sigmoid(x):
    return 1.0 / (1.0 + jnp.exp(-x))


def _softplus(x):
    return jnp.maximum(x, 0.0) + jnp.log(1.0 + jnp.exp(-jnp.abs(x)))


def _silu(x):
    return x * _sigmoid(x)


def _tile_rows(x, n):
    return jnp.concatenate([x] * n, axis=0)


def _block_diag_mask(rows, cols, rblk, cblk):
    r = lax.broadcasted_iota(jnp.int32, (rows, cols), 0) // rblk
    c = lax.broadcasted_iota(jnp.int32, (rows, cols), 1) // cblk
    return r == c


def _params(sem):
    return pltpu.CompilerParams(dimension_semantics=sem, vmem_limit_bytes=VMEM_LIMIT)


def _inproj_kernel(x_ref, nw_ref, w1_ref, w2_ref, w3_ref, o1_ref, o2_ref, o3_ref):
    x = x_ref[...]
    ms = jnp.mean(x * x, axis=-1, keepdims=True)
    h = (x * lax.rsqrt(ms + RMS_EPS) * nw_ref[...]).astype(MXU_DTYPE)
    o1_ref[...] = jnp.dot(h, w1_ref[...], preferred_element_type=F32)
    o2_ref[...] = jnp.dot(h, w2_ref[...], preferred_element_type=F32)
    o3_ref[...] = jnp.dot(h, w3_ref[...], preferred_element_type=F32)


def _inproj(xf, nw, w1, w2, w3, tm):
    t, d = xf.shape
    n1, n2, n3 = w1.shape[1], w2.shape[1], w3.shape[1]
    row = lambda i: (i, 0)
    fixed = lambda i: (0, 0)
    return pl.pallas_call(
        _inproj_kernel,
        grid=(t // tm,),
        in_specs=[pl.BlockSpec((tm, d), row), pl.BlockSpec((1, d), fixed),
                  pl.BlockSpec((d, n1), fixed), pl.BlockSpec((d, n2), fixed),
                  pl.BlockSpec((d, n3), fixed)],
        out_specs=[pl.BlockSpec((tm, n1), row), pl.BlockSpec((tm, n2), row),
                   pl.BlockSpec((tm, n3), row)],
        out_shape=[jax.ShapeDtypeStruct((t, n1), F32), jax.ShapeDtypeStruct((t, n2), F32),
                   jax.ShapeDtypeStruct((t, n3), F32)],
        compiler_params=_params(("arbitrary",)),
        name="inproj",
    )(xf, nw, w1, w2, w3)


def _gla_level_matrix():
    i = np.arange(CHUNK)[:, None]
    t = np.arange(CHUNK)[None, :]
    mats = [(t <= i)]
    for lvl in range(CHUNK_LEVELS):
        r = ((i >> (lvl + 1)) << (lvl + 1)) + (1 << lvl) - 1
        mats.append((t > np.minimum(i, r)) & (t <= np.maximum(i, r)))
    return np.concatenate(mats, axis=0).astype(np.float32)


def _gla_kernel(q_ref, k_ref, v_ref, z_ref, a_ref, wg_ref, bg_ref, nw_ref, lvl_ref,
                o_ref, st_ref, *, rows):
    @pl.when(pl.program_id(2) == 0)
    def _():
        st_ref[...] = jnp.zeros_like(st_ref)

    pre = _mm(a_ref[...], wg_ref[...]) + bg_ref[...]
    log_a = -_softplus(-pre) * (1.0 / GLA_GATE_TAU)

    lane = lax.broadcasted_iota(jnp.int32, (CHUNK, LANES), 1)
    head0 = lane < HEAD_DIM
    ridx = lax.broadcasted_iota(jnp.int32, (CHUNK, LANES), 0)
    head_ones = _block_diag_mask(LANES, LANES, HEAD_DIM, HEAD_DIM)
    head_ones_f = head_ones.astype(MXU_DTYPE)
    ri2 = lax.broadcasted_iota(jnp.int32, (2 * CHUNK, CHUNK), 0) % CHUNK
    ci2 = lax.broadcasted_iota(jnp.int32, (2 * CHUNK, CHUNK), 1)
    same_blk = [(ri2 >> (l + 1)) == (ci2 >> (l + 1)) for l in range(CHUNK_LEVELS)]
    upper = [((ridx >> l) & 1) == 1 for l in range(CHUNK_LEVELS)]
    lvl = lvl_ref[...]

    st = st_ref[...]
    outs = []
    for c in range(rows // CHUNK):
        sl = slice(c * CHUNK, (c + 1) * CHUNK)
        q = q_ref[sl, :] * (HEAD_DIM ** -0.5)
        k = k_ref[sl, :]
        v = v_ref[sl, :]
        dall = _mm_exact_rhs(lvl, log_a[sl, :])
        eall = jnp.exp(dall)
        cum = dall[0:CHUNK]
        last = cum[CHUNK - 1:CHUNK, :]
        q_dec = q * eall[0:CHUNK]
        k_dec = k * jnp.exp(last - cum)
        e_last = eall[CHUNK - 1:CHUNK, :]

        a_st = jnp.zeros((2 * CHUNK, CHUNK), F32)
        for l in range(CHUNK_LEVELS):
            e = eall[(l + 1) * CHUNK:(l + 2) * CHUNK]
            ql = jnp.where(upper[l], q * e, 0.0)
            kl = jnp.where(upper[l], 0.0, k * e)
            lhs = jnp.concatenate([jnp.where(head0, ql, 0.0), jnp.where(head0, 0.0, ql)], axis=0)
            a_st = a_st + jnp.where(same_blk[l], _mm_nt(lhs, kl), 0.0)
        o_full = _mm(a_st, v)
        o_intra = jnp.where(head0, o_full[0:CHUNK], o_full[CHUNK:2 * CHUNK])
        diag = _mm(q * k, head_ones_f)
        outs.append(o_intra + diag * v + _mm_nt(q_dec, st))
        st = st * e_last + jnp.where(head_ones, _mm_tn(v, k_dec), 0.0)
    st_ref[...] = st

    o_all = jnp.concatenate(outs, axis=0)
    ms = _mm(o_all * o_all, head_ones_f) * (1.0 / HEAD_DIM)
    y = o_all * lax.rsqrt(ms + RMS_EPS) * nw_ref[...]
    o_ref[...] = y * _silu(z_ref[...])


def _gla(pg, wg, bg, nw, lvl, batch, seq, rows):
    t = pg.shape[0]
    nt = seq // rows
    pairs = GLA_W // LANES
    col = lambda off: (lambda b, p, i: (b * nt + i, off + p))
    fixed = lambda b, p, i: (0, 0)
    blk = lambda off: pl.BlockSpec((rows, LANES), col(off))
    return pl.pallas_call(
        functools.partial(_gla_kernel, rows=rows),
        grid=(batch, pairs, nt),
        in_specs=[blk(0), blk(pairs), blk(2 * pairs), blk(3 * pairs),
                  pl.BlockSpec((rows, LANES), lambda b, p, i: (b * nt + i, 4 * pairs)),
                  pl.BlockSpec((LANES, LANES), lambda b, p, i: (0, p)),
                  pl.BlockSpec((1, LANES), lambda b, p, i: (0, p)),
                  pl.BlockSpec((1, LANES), fixed),
                  pl.BlockSpec(lvl.shape, fixed)],
        out_specs=pl.BlockSpec((rows, LANES), lambda b, p, i: (b * nt + i, p)),
        out_shape=jax.ShapeDtypeStruct((t, GLA_W), F32),
        scratch_shapes=[pltpu.VMEM((LANES, LANES), F32)],
        compiler_params=_params(("arbitrary", "arbitrary", "arbitrary")),
        name="gla",
    )(pg, pg, pg, pg, pg, wg, bg, nw, lvl)


def _gdn_kernel(q_ref, k_ref, v_ref, z_ref, e_ref, cq_ref, ck_ref, cv_ref, alog_ref, dt_ref,
                nw_ref, tri_ref, indb_ref, indg_ref, o_ref, bq_ref, bk_ref, bv_ref, st_ref,
                *, rows):
    w = GDN_W
    halo = 8

    @pl.when(pl.program_id(1) == 0)
    def _():
        st_ref[...] = jnp.zeros_like(st_ref)
        for buf in (bq_ref, bk_ref, bv_ref):
            buf[0:halo, :] = jnp.zeros((halo, w), F32)

    def conv_silu(x_ref, buf, cw_ref):
        buf[halo:halo + rows, :] = x_ref[...]
        acc = jnp.zeros((rows, w), F32)
        for i in range(CONV_WIDTH):
            acc = acc + cw_ref[i:i + 1, :] * buf[pl.ds(halo - (CONV_WIDTH - 1) + i, rows), :]
        buf[0:halo, :] = buf[rows:rows + halo, :]
        return _silu(acc)

    q = conv_silu(q_ref, bq_ref, cq_ref)
    k = conv_silu(k_ref, bk_ref, ck_ref)
    v = conv_silu(v_ref, bv_ref, cv_ref)

    head_ones = _block_diag_mask(w, w, HEAD_DIM, HEAD_DIM)
    head_ones_f = head_ones.astype(MXU_DTYPE)

    def sumsq(x):
        hi, lo = _split2(x * x)
        return _mm(hi, head_ones_f) + _mm(lo, head_ones_f)

    q = q * lax.rsqrt(sumsq(q) + RMS_EPS) * (HEAD_DIM ** -0.5)
    k = k * lax.rsqrt(sumsq(k) + RMS_EPS)

    extra = e_ref[...]
    beta_e = _mm_exact_lhs3(_sigmoid(extra), indb_ref[...])
    g_small = -jnp.exp(alog_ref[...]) * _softplus(extra + dt_ref[...])

    lane = lax.broadcasted_iota(jnp.int32, (CHUNK, w), 1) % HEAD_DIM
    ridx = lax.broadcasted_iota(jnp.int32, (CHUNK, w), 0)
    incl = lane <= ridx
    strict = lane < ridx
    r2 = lax.broadcasted_iota(jnp.int32, (w, w), 0)
    c2 = lax.broadcasted_iota(jnp.int32, (w, w), 1)
    eye = (r2 == c2).astype(F32)
    tri = tri_ref[...]
    indg = indg_ref[...]

    st = st_ref[...]
    outs = []
    for c in range(rows // CHUNK):
        sl = slice(c * CHUNK, (c + 1) * CHUNK)
        qc, kc, vc, bc = q[sl], k[sl], v[sl], beta_e[sl]
        h3 = _split3(g_small[sl])
        gc_small = _mm(tri, h3[0]) + _mm(tri, h3[1]) + _mm(tri, h3[2])
        gc = _mm_exact_lhs3(gc_small, indg)
        eg = jnp.exp(gc)
        g_last = gc[CHUNK - 1:CHUNK, :]
        kb = kc * bc
        vb = vc * bc
        kbg = kb * eg
        qg = qc * eg
        k_dec = kc * jnp.exp(g_last - gc)
        e_last = jnp.exp(g_last)

        g_hi, g_mid, g_lo = [p.astype(F32) for p in _split3(gc)]
        one = jnp.ones_like(gc)
        zero = jnp.zeros_like(gc)
        a2 = jnp.where(lane == 0, g_hi, jnp.where(lane == 1, g_mid, jnp.where(
            lane == 2, g_lo, jnp.where(lane < 6, one, zero))))
        b2 = jnp.where(lane < 3, one, jnp.where(lane == 3, -g_hi, jnp.where(
            lane == 4, -g_mid, jnp.where(lane == 5, -g_lo, zero))))
        diff = _mm_nt(a2, jnp.where(head_ones, _tile_rows(b2, GDN_HEADS), 0.0))
        decay = jnp.exp(jnp.where(incl, diff, NEG_BIG))

        k_bd = jnp.where(head_ones, _tile_rows(kc, GDN_HEADS), 0.0)
        kk = _mm_nt(jnp.concatenate([kb, qc], axis=0), k_bd)
        lower = jnp.where(strict, kk[0:CHUNK] * decay, 0.0)
        attn = kk[CHUNK:2 * CHUNK] * decay

        xk = jnp.where(head_ones, _tile_rows(-lower, GDN_HEADS), 0.0)
        p = eye + xk
        for _ in range(CHUNK_LEVELS - 1):
            xk = _mm(xk, xk)
            p = p + _mm(xk, p)
        t_all = p[0:CHUNK]
        for h in range(1, GDN_HEADS):
            t_all = t_all + p[h * CHUNK:(h + 1) * CHUNK]

        rhs = jnp.concatenate([jnp.where(head_ones, _tile_rows(vb, GDN_HEADS), 0.0),
                               jnp.where(head_ones, _tile_rows(kbg, GDN_HEADS), 0.0)], axis=1)
        uw = _mm(t_all, rhs)
        u, wmat = uw[:, 0:w], uw[:, w:2 * w]

        r = _mm_nt(jnp.concatenate([wmat, qg], axis=0), st)
        v_new = u - r[0:CHUNK]
        vn_bd = jnp.where(head_ones, _tile_rows(v_new, GDN_HEADS), 0.0)
        outs.append(r[CHUNK:2 * CHUNK] + _mm(attn, vn_bd))
        st = st * e_last + jnp.where(head_ones, _mm_tn(v_new, k_dec), 0.0)
    st_ref[...] = st

    o_all = jnp.concatenate(outs, axis=0)
    ms = _mm(o_all * o_all, head_ones_f) * (1.0 / HEAD_DIM)
    y = o_all * lax.rsqrt(ms + RMS_EPS) * nw_ref[...]
    o_ref[...] = y * _silu(z_ref[...])


def _gdn(pd, cq, ck, cv, alog, dt, nw, tri, indb, indg, batch, seq, rows):
    t = pd.shape[0]
    nt = seq // rows
    w = GDN_W
    fixed = lambda b, i: (0, 0)
    blk = lambda off: pl.BlockSpec((rows, w), lambda b, i: (b * nt + i, off))
    small = lambda a: pl.BlockSpec(a.shape, fixed)
    return pl.pallas_call(
        functools.partial(_gdn_kernel, rows=rows),
        grid=(batch, nt),
        in_specs=[blk(0), blk(1), blk(2), blk(3),
                  pl.BlockSpec((rows, LANES), lambda b, i: (b * nt + i, 4 * w // LANES)),
                  small(cq), small(ck), small(cv), small(alog), small(dt), small(nw),
                  small(tri), small(indb), small(indg)],
        out_specs=pl.BlockSpec((rows, w), lambda b, i: (b * nt + i, 0)),
        out_shape=jax.ShapeDtypeStruct((t, w), F32),
        scratch_shapes=[pltpu.VMEM((rows + 8, w), F32), pltpu.VMEM((rows + 8, w), F32),
                        pltpu.VMEM((rows + 8, w), F32), pltpu.VMEM((w, w), F32)],
        compiler_params=_params(("arbitrary", "arbitrary")),
        name="gdn",
    )(pd, pd, pd, pd, pd, cq, ck, cv, alog, dt, nw, tri, indb, indg)


def _moba_prep_kernel(q_ref, k_ref, v_ref, cos_ref, sin_ref, p1_ref, p2_ref,
                      qa_ref, ka_ref, vt_ref, km_ref, *, nblk):
    n = pl.program_id(1)
    blk = MOBA_BLOCK
    hw = MOBA_W

    @pl.when(n == 0)
    def _():
        km_ref[...] = jnp.zeros_like(km_ref)

    cosf = cos_ref[...]
    sinf = sin_ref[...]
    lane = lax.broadcasted_iota(jnp.int32, (blk, LANES), 1)
    first_half = (lane % HEAD_DIM) < (HEAD_DIM // 2)

    def rope(x_ref):
        parts = []
        for c in range(hw // LANES):
            x = x_ref[:, c * LANES:(c + 1) * LANES]
            swapped = jnp.where(first_half, pltpu.roll(x, LANES - HEAD_DIM // 2, 1),
                                pltpu.roll(x, HEAD_DIM // 2, 1))
            parts.append(x * cosf + swapped * sinf)
        return jnp.concatenate(parts, axis=1)

    q = rope(q_ref) * (HEAD_DIM ** -0.5 * LOG2_E)
    k = rope(k_ref)

    km_rows = jnp.where(_block_diag_mask(MOBA_HEADS * nblk, hw, nblk, HEAD_DIM),
                        _tile_rows(km_ref[...], MOBA_HEADS), 0.0)
    gate = lax.dot_general(km_rows, q, (((1,), (1,)), ((), ())),
                           precision=lax.Precision.HIGHEST, preferred_element_type=F32)
    gate = gate.reshape(MOBA_HEADS, nblk, blk)
    kb = lax.broadcasted_iota(jnp.int32, (MOBA_HEADS, nblk, blk), 1)
    past = kb < n
    g = jnp.where(past, gate, -jnp.inf)
    sel = jnp.zeros((MOBA_HEADS, nblk, blk), jnp.bool_)
    for _ in range(MOBA_TOPK):
        m = jnp.max(g, axis=1, keepdims=True)
        first = jnp.min(jnp.where(g == m, kb, nblk), axis=1, keepdims=True)
        pick = kb == first
        sel = sel | (pick & past)
        g = jnp.where(pick, -jnp.inf, g)
    bias_t = jnp.where(sel, 0.0, NEG_BIG).reshape(MOBA_HEADS * nblk, blk)
    bias = bias_t.T

    qa_ref[...] = (_mm(q, p1_ref[...]) + _mm(bias, p2_ref[...])).astype(qa_ref.dtype)
    lane_a = lax.broadcasted_iota(jnp.int32, (blk, MOBA_HEADS * LANES), 1) % LANES
    onehot = (lane_a == HEAD_DIM + n).astype(F32)
    ka_ref[...] = (_mm(k, p1_ref[...]) + onehot).astype(ka_ref.dtype)
    v_t = v_ref[...].T
    ones_row = (lax.broadcasted_iota(jnp.int32, (LANES - HEAD_DIM, blk), 0) == 0).astype(F32)
    vt_ref[...] = jnp.concatenate(
        [part for h in range(MOBA_HEADS)
         for part in (v_t[h * HEAD_DIM:(h + 1) * HEAD_DIM], ones_row)], axis=0).astype(vt_ref.dtype)
    km_ref[pl.ds(n, 1), :] = jnp.mean(k, axis=0, keepdims=True)


def _moba_prep(pm, cos, sin, p1, p2, batch, seq):
    t = pm.shape[0]
    nblk = seq // MOBA_BLOCK
    hw = MOBA_W
    aw = MOBA_HEADS * LANES
    fixed = lambda b, i: (0, 0)
    blk = lambda off: pl.BlockSpec((MOBA_BLOCK, hw), lambda b, i: (b * nblk + i, off))
    tab = pl.BlockSpec((MOBA_BLOCK, LANES), lambda b, i: (i, 0))
    row_out = pl.BlockSpec((MOBA_BLOCK, aw), lambda b, i: (b * nblk + i, 0))
    return pl.pallas_call(
        functools.partial(_moba_prep_kernel, nblk=nblk),
        grid=(batch, nblk),
        in_specs=[blk(0), blk(1), blk(2), tab, tab,
                  pl.BlockSpec(p1.shape, fixed), pl.BlockSpec(p2.shape, fixed)],
        out_specs=[row_out, row_out,
                   pl.BlockSpec((None, None, aw, MOBA_BLOCK), lambda b, i: (b, i, 0, 0))],
        out_shape=[jax.ShapeDtypeStruct((t, aw), MXU_DTYPE),
                   jax.ShapeDtypeStruct((t, aw), MXU_DTYPE),
                   jax.ShapeDtypeStruct((batch, nblk, aw, MOBA_BLOCK), MXU_DTYPE)],
        scratch_shapes=[pltpu.VMEM((nblk, hw), F32)],
        compiler_params=_params(("arbitrary", "arbitrary")),
        name="moba_prep",
    )(pm, pm, pm, cos, sin, p1, p2)


def _moba_attn_kernel(qa_ref, ka_ref, vt_ref, o_ref):
    n = pl.program_id(2)
    blk = MOBA_BLOCK
    two = 2 * blk
    heads = range(2)
    npair = (n + 1) // 2
    own = pl.multiple_of(n * blk, blk)
    key = lax.broadcasted_iota(jnp.int32, (blk, blk), 0)
    qry = lax.broadcasted_iota(jnp.int32, (blk, blk), 1)
    lane = lax.broadcasted_iota(jnp.int32, (blk, LANES), 1)
    hl = lambda h: slice(h * LANES, (h + 1) * LANES)

    s_own = []
    for h in heads:
        q = qa_ref[:, hl(h)]
        q_plain = jnp.where(lane < HEAD_DIM, q, jnp.zeros_like(q))
        s = _mm_nt(ka_ref[pl.ds(own, blk), hl(h)], q_plain)
        s_own.append(jnp.concatenate([jnp.where(key <= qry, s, NEG_BIG),
                                      jnp.full((blk, blk), NEG_BIG, F32)], axis=0))

    def softmax_step(s_pend, blk_a, blk_b, carry):
        new = []
        for h in heads:
            m, acc = carry[2 * h], carry[2 * h + 1]
            s = s_pend[h]
            m_new = jnp.maximum(m, jnp.max(s, axis=0, keepdims=True))
            p = jnp.exp2(s - m_new)
            vt = jnp.concatenate([vt_ref[blk_a, hl(h), :], vt_ref[blk_b, hl(h), :]], axis=1)
            new += [m_new, jnp.exp2(m - m_new) * acc + _mm(vt, p)]
        return tuple(new)

    def pending(i):
        return jnp.where(i == 0, n, 2 * i - 2), jnp.where(i == 0, n, 2 * i - 1)

    def body(i, state):
        s_pend, carry = state
        off = pl.multiple_of(i * two, two)
        s_next = tuple(_mm_nt(ka_ref[pl.ds(off, two), hl(h)], qa_ref[:, hl(h)]) for h in heads)
        return s_next, softmax_step(s_pend, *pending(i), carry)

    init = (jnp.full((1, blk), NEG_BIG, F32), jnp.zeros((LANES, blk), F32)) * 2
    s_pend, carry = lax.fori_loop(0, npair, body, (tuple(s_own), init))
    carry = softmax_step(s_pend, *pending(npair), carry)
    o_t = jnp.concatenate(
        [carry[2 * h + 1][0:HEAD_DIM] / carry[2 * h + 1][HEAD_DIM:HEAD_DIM + 1] for h in heads],
        axis=0)
    o_ref[...] = o_t.T


def _moba_attn(qa, ka, vt, batch, seq):
    t = qa.shape[0]
    nblk = seq // MOBA_BLOCK
    pairs = MOBA_HEADS // 2
    return pl.pallas_call(
        _moba_attn_kernel,
        grid=(batch, pairs, nblk),
        in_specs=[pl.BlockSpec((MOBA_BLOCK, 2 * LANES), lambda b, p, i: (b * nblk + i, p)),
                  pl.BlockSpec((seq, 2 * LANES), lambda b, p, i: (b, p)),
                  pl.BlockSpec((None, nblk, 2 * LANES, MOBA_BLOCK), lambda b, p, i: (b, 0, p, 0))],
        out_specs=pl.BlockSpec((MOBA_BLOCK, LANES), lambda b, p, i: (b * nblk + i, p)),
        out_shape=jax.ShapeDtypeStruct((t, MOBA_W), F32),
        compiler_params=_params(("arbitrary", "arbitrary", "arbitrary")),
        name="moba_attn",
    )(qa, ka, vt)


def _outproj_kernel(og_ref, od_ref, om_ref, x_ref, w1_ref, w2_ref, w3_ref, nw_ref, o_ref):
    mix = (_mm(og_ref[...], w1_ref[...]) + _mm(od_ref[...], w2_ref[...])
           + _mm(om_ref[...], w3_ref[...]))
    ms = jnp.mean(mix * mix, axis=-1, keepdims=True)
    o_ref[...] = x_ref[...] + mix * lax.rsqrt(ms + RMS_EPS) * nw_ref[...]


def _outproj(og, od, om, xf, w1, w2, w3, nw, tm):
    t, d = xf.shape
    row = lambda i: (i, 0)
    fixed = lambda i: (0, 0)
    full = lambda a: pl.BlockSpec(a.shape, fixed)
    tile = lambda a: pl.BlockSpec((tm, a.shape[1]), row)
    return pl.pallas_call(
        _outproj_kernel,
        grid=(t // tm,),
        in_specs=[tile(og), tile(od), tile(om), tile(xf), full(w1), full(w2), full(w3), full(nw)],
        out_specs=pl.BlockSpec((tm, d), row),
        out_shape=jax.ShapeDtypeStruct((t, d), F32),
        compiler_params=_params(("arbitrary",)),
        name="outproj",
    )(og, od, om, xf, w1, w2, w3, nw)


def _ffn_kernel(x_ref, npre_ref, wg_ref, wu_ref, wd_ref, npost_ref, o_ref, h_ref, acc_ref):
    f = pl.program_id(1)

    @pl.when(f == 0)
    def _():
        x = x_ref[...]
        ms = jnp.mean(x * x, axis=-1, keepdims=True)
        h_ref[...] = (x * lax.rsqrt(ms + RMS_EPS) * npre_ref[...]).astype(h_ref.dtype)
        acc_ref[...] = jnp.zeros_like(acc_ref)

    h = h_ref[...]
    g = jnp.dot(h, wg_ref[...], preferred_element_type=F32)
    u = jnp.dot(h, wu_ref[...], preferred_element_type=F32)
    acc_ref[...] += _mm(_silu(g) * u, wd_ref[...])

    @pl.when(f == pl.num_programs(1) - 1)
    def _():
        y = acc_ref[...]
        ms = jnp.mean(y * y, axis=-1, keepdims=True)
        o_ref[...] = x_ref[...] + y * lax.rsqrt(ms + RMS_EPS) * npost_ref[...]


def _ffn(xf, npre, wg, wu, wd, npost, tm, tf):
    t, d = xf.shape
    dff = wg.shape[1]
    return pl.pallas_call(
        _ffn_kernel,
        grid=(t // tm, dff // tf),
        in_specs=[pl.BlockSpec((tm, d), lambda i, f: (i, 0)),
                  pl.BlockSpec((1, d), lambda i, f: (0, 0)),
                  pl.BlockSpec((d, tf), lambda i, f: (0, f)),
                  pl.BlockSpec((d, tf), lambda i, f: (0, f)),
                  pl.BlockSpec((tf, d), lambda i, f: (f, 0)),
                  pl.BlockSpec((1, d), lambda i, f: (0, 0))],
        out_specs=pl.BlockSpec((tm, d), lambda i, f: (i, 0)),
        out_shape=jax.ShapeDtypeStruct((t, d), F32),
        scratch_shapes=[pltpu.VMEM((tm, d), MXU_DTYPE), pltpu.VMEM((tm, d), F32)],
        compiler_params=_params(("arbitrary", "arbitrary")),
        name="ffn",
    )(xf, npre, wg, wu, wd, npost)


def _tile_sizes(tokens, seq):
    pick = lambda want, n: want if n % want == 0 else MOBA_BLOCK
    return dict(proj=pick(512, tokens), ffn=pick(1024, tokens), rec=pick(256, seq), ffn_cols=256)


def _rope_tables(seq):
    half = HEAD_DIM // 2
    inv = ROPE_THETA ** (-jnp.arange(half, dtype=F32) / half)
    ang = jnp.arange(seq).astype(F32)[:, None] * inv[None, :]
    cos, sin = jnp.cos(ang), jnp.sin(ang)
    reps = LANES // HEAD_DIM
    return (jnp.tile(jnp.concatenate([cos, cos], axis=1), (1, reps)),
            jnp.tile(jnp.concatenate([-sin, sin], axis=1), (1, reps)))


def _moba_placement(nblk):
    aw = MOBA_HEADS * LANES
    p1 = np.zeros((MOBA_W, aw), np.float32)
    p2 = np.zeros((MOBA_HEADS * nblk, aw), np.float32)
    for h in range(MOBA_HEADS):
        for d in range(HEAD_DIM):
            p1[h * HEAD_DIM + d, h * LANES + d] = 1.0
        for n in range(nblk):
            p2[h * nblk + n, h * LANES + HEAD_DIM + n] = 1.0
    return jnp.asarray(p1, MXU_DTYPE), jnp.asarray(p2, MXU_DTYPE)


def _head_indicator(first_lane, heads):
    ind = np.zeros((LANES, heads * HEAD_DIM), np.float32)
    for h in range(heads):
        ind[first_lane + h, h * HEAD_DIM:(h + 1) * HEAD_DIM] = 1.0
    return jnp.asarray(ind, MXU_DTYPE)


def _lane_row(values, first_lane):
    return jnp.zeros((1, LANES), F32).at[0, first_lane:first_lane + values.shape[0]].set(
        values.astype(F32))


def kernel(x, norm_mix_pre, norm_mix_post, norm_ffn_pre, norm_ffn_post, w_in, w_o, gla_w_gate,
           gla_b_gate, gla_norm, gdn_conv, gdn_a_log, gdn_dt_bias, gdn_norm, ffn_w_gate, ffn_w_up,
           ffn_w_down):
    batch, seq, d_model = x.shape
    depth = w_in.shape[0]
    tokens = batch * seq
    nblk = seq // MOBA_BLOCK
    assert seq % MOBA_BLOCK == 0 and nblk + HEAD_DIM <= LANES
    ts = _tile_sizes(tokens, seq)

    cos, sin = _rope_tables(seq)
    p1, p2 = _moba_placement(nblk)
    lvl = jnp.asarray(_gla_level_matrix(), MXU_DTYPE)
    tri = jnp.asarray(np.tril(np.ones((CHUNK, CHUNK), np.float32)), MXU_DTYPE)
    indb = _head_indicator(0, GDN_HEADS)
    indg = _head_indicator(GDN_HEADS, GDN_HEADS)
    row2d = lambda a: a.reshape(1, -1).astype(F32)
    pad_cols = lambda a, n: jnp.pad(a, ((0, 0), (0, n - a.shape[1])))
    bf = lambda a: a.astype(MXU_DTYPE)

    o_ga = 4 * GLA_W
    o_d = o_ga + GLA_GATE_RANK
    o_db = o_d + 4 * GDN_W
    o_m = o_db + 2 * GDN_HEADS

    xf = x.reshape(tokens, d_model)
    for l in range(depth):
        wl = w_in[l]
        w_gla = bf(jnp.concatenate([wl[:, :o_ga], pad_cols(wl[:, o_ga:o_d], LANES)], axis=1))
        w_gdn = bf(jnp.concatenate([wl[:, o_d:o_db], pad_cols(wl[:, o_db:o_m], LANES)], axis=1))
        w_moba = bf(wl[:, o_m:])
        pg, pd, pm = _inproj(xf, row2d(norm_mix_pre[l]), w_gla, w_gdn, w_moba, ts["proj"])

        wgate = bf(jnp.pad(gla_w_gate[l], ((0, LANES - GLA_GATE_RANK), (0, 0))))
        o_gla = _gla(pg, wgate, row2d(gla_b_gate[l]),
                     row2d(jnp.tile(gla_norm[l], LANES // HEAD_DIM)), lvl, batch, seq, ts["rec"])

        conv = gdn_conv[l].astype(F32)
        o_gdn = _gdn(pd, conv[:, :GDN_W], conv[:, GDN_W:2 * GDN_W], conv[:, 2 * GDN_W:],
                     _lane_row(gdn_a_log[l], GDN_HEADS), _lane_row(gdn_dt_bias[l], GDN_HEADS),
                     row2d(jnp.tile(gdn_norm[l], GDN_HEADS)), tri, indb, indg,
                     batch, seq, ts["rec"])

        qa, ka, vb = _moba_prep(pm, cos, sin, p1, p2, batch, seq)
        o_moba = _moba_attn(qa, ka, vb, batch, seq)

        wo = bf(w_o[l])
        xf = _outproj(o_gla, o_gdn, o_moba, xf, wo[:GLA_W], wo[GLA_W:GLA_W + GDN_W],
                      wo[GLA_W + GDN_W:], row2d(norm_mix_post[l]), ts["proj"])
        xf = _ffn(xf, row2d(norm_ffn_pre[l]), bf(ffn_w_gate[l]), bf(ffn_w_up[l]),
                  bf(ffn_w_down[l]), row2d(norm_ffn_post[l]), ts["ffn"], ts["ffn_cols"])
    return xf.reshape(batch, seq, d_model)
```

```python
import functools

import numpy as np
import jax
import jax.numpy as jnp
from jax import lax
from jax.experimental import pallas as pl
from jax.experimental.pallas import tpu as pltpu

F32 = jnp.float32
MXU_DTYPE = jnp.bfloat16

HEAD_DIM = 64
GLA_HEADS = 4
GDN_HEADS = 4
MOBA_HEADS = 8
GLA_W = GLA_HEADS * HEAD_DIM
GDN_W = GDN_HEADS * HEAD_DIM
MOBA_W = MOBA_HEADS * HEAD_DIM
GLA_GATE_RANK = 16
GLA_GATE_TAU = 16.0
CHUNK = 64
CHUNK_LEVELS = 6
CONV_WIDTH = 4
MOBA_BLOCK = 256
MOBA_TOPK = 3
ROPE_THETA = 10000.0
RMS_EPS = 1e-6
LANES = 128
NEG_BIG = -1e30
LOG2_E = 1.4426950408889634
VMEM_LIMIT = 48 * 1024 * 1024


def _mm(a, b):
    return jnp.dot(a.astype(MXU_DTYPE), b.astype(MXU_DTYPE), preferred_element_type=F32)


def _mm_nt(a, b):
    return lax.dot_general(a.astype(MXU_DTYPE), b.astype(MXU_DTYPE),
                           (((1,), (1,)), ((), ())), preferred_element_type=F32)


def _mm_tn(a, b):
    return lax.dot_general(a.astype(MXU_DTYPE), b.astype(MXU_DTYPE),
                           (((0,), (0,)), ((), ())), preferred_element_type=F32)


def _split2(x):
    hi = x.astype(MXU_DTYPE)
    lo = (x - hi.astype(F32)).astype(MXU_DTYPE)
    return hi, lo


def _split3(x):
    hi = x.astype(MXU_DTYPE)
    r = x - hi.astype(F32)
    mid = r.astype(MXU_DTYPE)
    lo = (r - mid.astype(F32)).astype(MXU_DTYPE)
    return hi, mid, lo


def _mm_exact_rhs(c, x):
    hi, lo = _split2(x)
    return _mm(c, hi) + _mm(c, lo)


def _mm_exact_lhs3(x, c):
    hi, mid, lo = _split3(x)
    return _mm(hi, c) + _mm(mid, c) + _mm(lo, c)


def _sigmoid(x):
    return 1.0 / (1.0 + jnp.exp(-x))


def _softplus(x):
    return jnp.maximum(x, 0.0) + jnp.log(1.0 + jnp.exp(-jnp.abs(x)))


def _silu(x):
    return x * _sigmoid(x)


def _tile_rows(x, n):
    return jnp.concatenate([x] * n, axis=0)


def _block_diag_mask(rows, cols, rblk, cblk):
    r = lax.broadcasted_iota(jnp.int32, (rows, cols), 0) // rblk
    c = lax.broadcasted_iota(jnp.int32, (rows, cols), 1) // cblk
    return r == c


def _params(sem):
    return pltpu.CompilerParams(dimension_semantics=sem, vmem_limit_bytes=VMEM_LIMIT)


def _inproj_kernel(x_ref, nw_ref, w1_ref, w2_ref, w3_ref, o1_ref, o2_ref, o3_ref):
    x = x_ref[...]
    ms = jnp.mean(x * x, axis=-1, keepdims=True)
    h = (x * lax.rsqrt(ms + RMS_EPS) * nw_ref[...]).astype(MXU_DTYPE)
    o1_ref[...] = jnp.dot(h, w1_ref[...], preferred_element_type=F32)
    o2_ref[...] = jnp.dot(h, w2_ref[...], preferred_element_type=F32)
    o3_ref[...] = jnp.dot(h, w3_ref[...], preferred_element_type=F32)


def _inproj(xf, nw, w1, w2, w3, tm):
    t, d = xf.shape
    n1, n2, n3 = w1.shape[1], w2.shape[1], w3.shape[1]
    row = lambda i: (i, 0)
    fixed = lambda i: (0, 0)
    return pl.pallas_call(
        _inproj_kernel,
        grid=(t // tm,),
        in_specs=[pl.BlockSpec((tm, d), row), pl.BlockSpec((1, d), fixed),
                  pl.BlockSpec((d, n1), fixed), pl.BlockSpec((d, n2), fixed),
                  pl.BlockSpec((d, n3), fixed)],
        out_specs=[pl.BlockSpec((tm, n1), row), pl.BlockSpec((tm, n2), row),
                   pl.BlockSpec((tm, n3), row)],
        out_shape=[jax.ShapeDtypeStruct((t, n1), F32), jax.ShapeDtypeStruct((t, n2), F32),
                   jax.ShapeDtypeStruct((t, n3), F32)],
        compiler_params=_params(("arbitrary",)),
        name="inproj",
    )(xf, nw, w1, w2, w3)


def _gla_level_decays(g, cum, ridx):
    prev = pltpu.roll(g, 1, 0)
    nxt = pltpu.roll(g, CHUNK - 1, 0)
    r4 = ridx % 4
    logs = [jnp.where(ridx % 2 == 1, g, 0.0),
            jnp.where(r4 == 0, nxt, jnp.where(r4 == 1, 0.0, jnp.where(r4 == 2, g, g + prev)))]
    for lvl in range(2, CHUNK_LEVELS):
        half = 1 << lvl
        ref = jnp.concatenate(
            [jnp.broadcast_to(cum[b + half - 1:b + half, :], (2 * half, LANES))
             for b in range(0, CHUNK, 2 * half)], axis=0)
        logs.append(-jnp.abs(cum - ref))
    return [jnp.exp(d) for d in logs]


def _gla_kernel(q_ref, k_ref, v_ref, z_ref, a_ref, wg_ref, bg_ref, nw_ref, tri_ref,
                o_ref, st_ref, *, rows):
    @pl.when(pl.program_id(2) == 0)
    def _():
        st_ref[...] = jnp.zeros_like(st_ref)

    pre = _mm(a_ref[...], wg_ref[...]) + bg_ref[...]
    log_a = -_softplus(-pre) * (1.0 / GLA_GATE_TAU)

    lane = lax.broadcasted_iota(jnp.int32, (CHUNK, LANES), 1)
    head0 = lane < HEAD_DIM
    ridx = lax.broadcasted_iota(jnp.int32, (CHUNK, LANES), 0)
    head_ones = _block_diag_mask(LANES, LANES, HEAD_DIM, HEAD_DIM)
    head_ones_f = head_ones.astype(MXU_DTYPE)
    ri2 = lax.broadcasted_iota(jnp.int32, (2 * CHUNK, CHUNK), 0) % CHUNK
    ci2 = lax.broadcasted_iota(jnp.int32, (2 * CHUNK, CHUNK), 1)
    same_blk = [(ri2 >> (l + 1)) == (ci2 >> (l + 1)) for l in range(CHUNK_LEVELS)]
    upper = [((ridx >> l) & 1) == 1 for l in range(CHUNK_LEVELS)]
    tri = tri_ref[...]

    st = st_ref[...]
    outs = []
    for c in range(rows // CHUNK):
        sl = slice(c * CHUNK, (c + 1) * CHUNK)
        q = q_ref[sl, :] * (HEAD_DIM ** -0.5)
        k = k_ref[sl, :]
        v = v_ref[sl, :]
        g = log_a[sl, :]
        cum = _mm_exact_rhs(tri, g)
        last = cum[CHUNK - 1:CHUNK, :]
        q_dec = q * jnp.exp(cum)
        k_dec = k * jnp.exp(last - cum)
        e_last = jnp.exp(last)
        decays = _gla_level_decays(g, cum, ridx)

        a_st = jnp.zeros((2 * CHUNK, CHUNK), F32)
        for l in range(CHUNK_LEVELS):
            e = decays[l]
            ql = jnp.where(upper[l], q * e, 0.0)
            kl = jnp.where(upper[l], 0.0, k * e)
            lhs = jnp.concatenate([jnp.where(head0, ql, 0.0), jnp.where(head0, 0.0, ql)], axis=0)
            a_st = a_st + jnp.where(same_blk[l], _mm_nt(lhs, kl), 0.0)
        o_full = _mm(a_st, v)
        o_intra = jnp.where(head0, o_full[0:CHUNK], o_full[CHUNK:2 * CHUNK])
        diag = _mm(q * k, head_ones_f)
        outs.append(o_intra + diag * v + _mm_nt(q_dec, st))
        st = st * e_last + jnp.where(head_ones, _mm_tn(v, k_dec), 0.0)
    st_ref[...] = st

    o_all = jnp.concatenate(outs, axis=0)
    ms = _mm(o_all * o_all, head_ones_f) * (1.0 / HEAD_DIM)
    y = o_all * lax.rsqrt(ms + RMS_EPS) * nw_ref[...]
    o_ref[...] = y * _silu(z_ref[...])


def _gla(pg, wg, bg, nw, tri, batch, seq, rows):
    t = pg.shape[0]
    nt = seq // rows
    pairs = GLA_W // LANES
    col = lambda off: (lambda b, p, i: (b * nt + i, off + p))
    fixed = lambda b, p, i: (0, 0)
    blk = lambda off: pl.BlockSpec((rows, LANES), col(off))
    return pl.pallas_call(
        functools.partial(_gla_kernel, rows=rows),
        grid=(batch, pairs, nt),
        in_specs=[blk(0), blk(pairs), blk(2 * pairs), blk(3 * pairs),
                  pl.BlockSpec((rows, LANES), lambda b, p, i: (b * nt + i, 4 * pairs)),
                  pl.BlockSpec((LANES, LANES), lambda b, p, i: (0, p)),
                  pl.BlockSpec((1, LANES), lambda b, p, i: (0, p)),
                  pl.BlockSpec((1, LANES), fixed),
                  pl.BlockSpec(tri.shape, fixed)],
        out_specs=pl.BlockSpec((rows, LANES), lambda b, p, i: (b * nt + i, p)),
        out_shape=jax.ShapeDtypeStruct((t, GLA_W), F32),
        scratch_shapes=[pltpu.VMEM((LANES, LANES), F32)],
        compiler_params=_params(("arbitrary", "arbitrary", "arbitrary")),
        name="gla",
    )(pg, pg, pg, pg, pg, wg, bg, nw, tri)


def _gdn_kernel(q_ref, k_ref, v_ref, z_ref, e_ref, cq_ref, ck_ref, cv_ref, alog_ref, dt_ref,
                nw_ref, tri_ref, indb_ref, indg_ref, o_ref, bq_ref, bk_ref, bv_ref, st_ref,
                *, rows):
    w = GDN_W
    halo = 8

    @pl.when(pl.program_id(1) == 0)
    def _():
        st_ref[...] = jnp.zeros_like(st_ref)
        for buf in (bq_ref, bk_ref, bv_ref):
            buf[0:halo, :] = jnp.zeros((halo, w), F32)

    def conv_silu(x_ref, buf, cw_ref):
        buf[halo:halo + rows, :] = x_ref[...]
        acc = jnp.zeros((rows, w), F32)
        for i in range(CONV_WIDTH):
            acc = acc + cw_ref[i:i + 1, :] * buf[pl.ds(halo - (CONV_WIDTH - 1) + i, rows), :]
        buf[0:halo, :] = buf[rows:rows + halo, :]
        return _silu(acc)

    q = conv_silu(q_ref, bq_ref, cq_ref)
    k = conv_silu(k_ref, bk_ref, ck_ref)
    v = conv_silu(v_ref, bv_ref, cv_ref)

    head_ones = _block_diag_mask(w, w, HEAD_DIM, HEAD_DIM)
    head_ones_f = head_ones.astype(MXU_DTYPE)

    def sumsq(x):
        hi, lo = _split2(x * x)
        return _mm(hi, head_ones_f) + _mm(lo, head_ones_f)

    q = q * lax.rsqrt(sumsq(q) + RMS_EPS) * (HEAD_DIM ** -0.5)
    k = k * lax.rsqrt(sumsq(k) + RMS_EPS)

    extra = e_ref[...]
    beta_e = _mm_exact_lhs3(_sigmoid(extra), indb_ref[...])
    g_small = -jnp.exp(alog_ref[...]) * _softplus(extra + dt_ref[...])

    lane = lax.broadcasted_iota(jnp.int32, (CHUNK, w), 1) % HEAD_DIM
    ridx = lax.broadcasted_iota(jnp.int32, (CHUNK, w), 0)
    incl = lane <= ridx
    strict = lane < ridx
    tri = tri_ref[...]
    indg = indg_ref[...]

    bd = lambda x: jnp.where(head_ones, _tile_rows(x, GDN_HEADS), 0.0)
    eye_all = (lane == ridx).astype(F32)
    nchunk = rows // CHUNK

    xs, ps, keep = [], [], []
    for c in range(nchunk):
        sl = slice(c * CHUNK, (c + 1) * CHUNK)
        qc, kc, vc, bc = q[sl], k[sl], v[sl], beta_e[sl]
        h3 = _split3(g_small[sl])
        gc_small = _mm(tri, h3[0]) + _mm(tri, h3[1]) + _mm(tri, h3[2])
        gc = _mm_exact_lhs3(gc_small, indg)
        eg = jnp.exp(gc)
        g_last = gc[CHUNK - 1:CHUNK, :]
        kb = kc * bc

        g_hi, g_mid, g_lo = [p.astype(F32) for p in _split3(gc)]
        one = jnp.ones_like(gc)
        zero = jnp.zeros_like(gc)
        a2 = jnp.where(lane == 0, g_hi, jnp.where(lane == 1, g_mid, jnp.where(
            lane == 2, g_lo, jnp.where(lane < 6, one, zero))))
        b2 = jnp.where(lane < 3, one, jnp.where(lane == 3, -g_hi, jnp.where(
            lane == 4, -g_mid, jnp.where(lane == 5, -g_lo, zero))))
        decay = jnp.exp(jnp.where(incl, _mm_nt(a2, bd(b2)), NEG_BIG))

        kk = _mm_nt(jnp.concatenate([kb, qc], axis=0), bd(kc))
        xs.append(-jnp.where(strict, kk[0:CHUNK] * decay, 0.0))
        ps.append(eye_all)
        keep.append(dict(attn=kk[CHUNK:2 * CHUNK] * decay, vb=vc * bc, kbg=kb * eg, qg=qc * eg,
                         k_dec=kc * jnp.exp(g_last - gc), e_last=jnp.exp(g_last)))

    for lvl in range(CHUNK_LEVELS):
        for c in range(nchunk):
            rhs = bd(xs[c])
            if lvl < CHUNK_LEVELS - 1:
                r = _mm(jnp.concatenate([xs[c], ps[c]], axis=0), rhs)
                xs[c] = r[0:CHUNK]
                ps[c] = ps[c] + r[CHUNK:2 * CHUNK]
            else:
                ps[c] = ps[c] + _mm(ps[c], rhs)

    steps = []
    for c in range(nchunk):
        kp = keep[c]
        uw = _mm(ps[c], jnp.concatenate([bd(kp["vb"]), bd(kp["kbg"])], axis=1))
        u, wmat = uw[:, 0:w], uw[:, w:2 * w]
        ao = _mm(kp["attn"], jnp.concatenate([bd(wmat), bd(u)], axis=1))
        nb = _mm_tn(jnp.concatenate([wmat, u], axis=1), kp["k_dec"])
        steps.append((kp["qg"] - ao[:, 0:w], ao[:, w:2 * w], kp["e_last"],
                      jnp.where(head_ones, nb[0:w], 0.0), jnp.where(head_ones, nb[w:2 * w], 0.0)))

    st = st_ref[...]
    outs = []
    for q_eff, o_const, e_last, n_mat, b_mat in steps:
        outs.append(_mm_nt(q_eff, st) + o_const)
        st = st * e_last - _mm(st, n_mat) + b_mat
    st_ref[...] = st

    o_all = jnp.concatenate(outs, axis=0)
    ms = _mm(o_all * o_all, head_ones_f) * (1.0 / HEAD_DIM)
    y = o_all * lax.rsqrt(ms + RMS_EPS) * nw_ref[...]
    o_ref[...] = y * _silu(z_ref[...])


def _gdn(pd, cq, ck, cv, alog, dt, nw, tri, indb, indg, batch, seq, rows):
    t = pd.shape[0]
    nt = seq // rows
    w = GDN_W
    fixed = lambda b, i: (0, 0)
    blk = lambda off: pl.BlockSpec((rows, w), lambda b, i: (b * nt + i, off))
    small = lambda a: pl.BlockSpec(a.shape, fixed)
    return pl.pallas_call(
        functools.partial(_gdn_kernel, rows=rows),
        grid=(batch, nt),
        in_specs=[blk(0), blk(1), blk(2), blk(3),
                  pl.BlockSpec((rows, LANES), lambda b, i: (b * nt + i, 4 * w // LANES)),
                  small(cq), small(ck), small(cv), small(alog), small(dt), small(nw),
                  small(tri), small(indb), small(indg)],
        out_specs=pl.BlockSpec((rows, w), lambda b, i: (b * nt + i, 0)),
        out_shape=jax.ShapeDtypeStruct((t, w), F32),
        scratch_shapes=[pltpu.VMEM((rows + 8, w), F32), pltpu.VMEM((rows + 8, w), F32),
                        pltpu.VMEM((rows + 8, w), F32), pltpu.VMEM((w, w), F32)],
        compiler_params=_params(("arbitrary", "arbitrary")),
        name="gdn",
    )(pd, pd, pd, pd, pd, cq, ck, cv, alog, dt, nw, tri, indb, indg)


def _moba_prep_kernel(q_ref, k_ref, v_ref, cos_ref, sin_ref,
                      qo_ref, ko_ref, vt_ref, bias_ref, km_ref, *, nblk):
    n = pl.program_id(1)
    blk = MOBA_BLOCK
    hw = MOBA_W

    @pl.when(n == 0)
    def _():
        km_ref[...] = jnp.zeros_like(km_ref)

    cosf = cos_ref[...]
    sinf = sin_ref[...]
    lane = lax.broadcasted_iota(jnp.int32, (blk, LANES), 1)
    first_half = (lane % HEAD_DIM) < (HEAD_DIM // 2)

    def rope(x_ref):
        parts = []
        for c in range(hw // LANES):
            x = x_ref[:, c * LANES:(c + 1) * LANES]
            swapped = jnp.where(first_half, pltpu.roll(x, LANES - HEAD_DIM // 2, 1),
                                pltpu.roll(x, HEAD_DIM // 2, 1))
            parts.append(x * cosf + swapped * sinf)
        return jnp.concatenate(parts, axis=1)

    q = rope(q_ref) * (HEAD_DIM ** -0.5 * LOG2_E)
    k = rope(k_ref)

    km_rows = jnp.where(_block_diag_mask(MOBA_HEADS * nblk, hw, nblk, HEAD_DIM),
                        _tile_rows(km_ref[...], MOBA_HEADS), 0.0)
    km_hi, km_lo = _split2(km_rows)
    q_hi, q_lo = _split2(q)
    gate = _mm_nt(km_hi, q_hi) + _mm_nt(km_hi, q_lo) + _mm_nt(km_lo, q_hi)
    gate = gate.reshape(MOBA_HEADS, nblk, blk)
    kb = lax.broadcasted_iota(jnp.int32, (MOBA_HEADS, nblk, blk), 1)
    past = kb < n
    g = jnp.where(past, gate, -jnp.inf)
    sel = jnp.zeros((MOBA_HEADS, nblk, blk), jnp.bool_)
    for _ in range(MOBA_TOPK):
        m = jnp.max(g, axis=1, keepdims=True)
        first = jnp.min(jnp.where(g == m, kb, nblk), axis=1, keepdims=True)
        pick = kb == first
        sel = sel | (pick & past)
        g = jnp.where(pick, -jnp.inf, g)
    bias_ref[...] = jnp.where(sel, 0.0, NEG_BIG).reshape(MOBA_HEADS * nblk, blk)

    qo_ref[...] = q.astype(qo_ref.dtype)
    ko_ref[...] = k.astype(ko_ref.dtype)
    v_t = v_ref[...].T
    ones_row = (lax.broadcasted_iota(jnp.int32, (LANES - HEAD_DIM, blk), 0) == 0).astype(F32)
    vt_ref[...] = jnp.concatenate(
        [part for h in range(MOBA_HEADS)
         for part in (v_t[h * HEAD_DIM:(h + 1) * HEAD_DIM], ones_row)], axis=0).astype(vt_ref.dtype)
    km_ref[pl.ds(n, 1), :] = jnp.mean(k, axis=0, keepdims=True)


def _moba_prep(pm, cos, sin, batch, seq):
    t = pm.shape[0]
    nblk = seq // MOBA_BLOCK
    hw = MOBA_W
    aw = MOBA_HEADS * LANES
    blk = lambda off: pl.BlockSpec((MOBA_BLOCK, hw), lambda b, i: (b * nblk + i, off))
    tab = pl.BlockSpec((MOBA_BLOCK, LANES), lambda b, i: (i, 0))
    row_out = pl.BlockSpec((MOBA_BLOCK, hw), lambda b, i: (b * nblk + i, 0))
    per_block = lambda r: pl.BlockSpec((None, None, r, MOBA_BLOCK), lambda b, i: (b, i, 0, 0))
    return pl.pallas_call(
        functools.partial(_moba_prep_kernel, nblk=nblk),
        grid=(batch, nblk),
        in_specs=[blk(0), blk(1), blk(2), tab, tab],
        out_specs=[row_out, row_out, per_block(aw), per_block(MOBA_HEADS * nblk)],
        out_shape=[jax.ShapeDtypeStruct((t, hw), MXU_DTYPE),
                   jax.ShapeDtypeStruct((t, hw), MXU_DTYPE),
                   jax.ShapeDtypeStruct((batch, nblk, aw, MOBA_BLOCK), MXU_DTYPE),
                   jax.ShapeDtypeStruct((batch, nblk, MOBA_HEADS * nblk, MOBA_BLOCK), F32)],
        scratch_shapes=[pltpu.VMEM((nblk, hw), F32)],
        compiler_params=_params(("arbitrary", "arbitrary")),
        name="moba_prep",
    )(pm, pm, pm, cos, sin)


def _moba_attn_kernel(q_ref, k_ref, vt_ref, bias_ref, o_ref, sa_ref, sb_ref, *, nblk):
    n = pl.program_id(2)
    blk = MOBA_BLOCK
    two = 2 * blk
    heads = range(2)
    npair = (n + 1) // 2
    own = pl.multiple_of(n * blk, blk)
    key = lax.broadcasted_iota(jnp.int32, (blk, blk), 0)
    qry = lax.broadcasted_iota(jnp.int32, (blk, blk), 1)
    lane = lax.broadcasted_iota(jnp.int32, (blk, LANES), 1)
    hl = lambda h: slice(h * LANES, (h + 1) * LANES)
    q_pair = q_ref[...]
    q_head = [jnp.where((lane // HEAD_DIM) == h, q_pair, jnp.zeros_like(q_pair)) for h in heads]

    k_own = k_ref[pl.ds(own, blk), :]
    for h in heads:
        s = _mm_nt(k_own, q_head[h])
        sa_ref[h, 0:blk, :] = jnp.where(key <= qry, s, NEG_BIG)
        sa_ref[h, blk:two, :] = jnp.full((blk, blk), NEG_BIG, F32)

    def produce(buf, item):
        pair = jnp.clip(item - 1, 0, nblk // 2 - 1)
        k_pair = k_ref[pl.ds(pl.multiple_of(pair * two, two), two), :]
        for h in heads:
            buf[h] = _mm_nt(k_pair, q_head[h])

    def consume(buf, item, carry):
        is_own = item == 0
        blk_a = jnp.where(is_own, n, jnp.clip(2 * item - 2, 0, nblk - 1))
        blk_b = jnp.where(is_own, n, jnp.clip(2 * item - 1, 0, nblk - 1))
        penalty = jnp.where(item <= npair, 0.0, NEG_BIG)
        new = []
        for h in heads:
            m, acc = carry[2 * h], carry[2 * h + 1]
            row_a = jnp.where(is_own, 0.0, bias_ref[pl.ds(h * nblk + blk_a, 1), :] + penalty)
            row_b = jnp.where(is_own, 0.0, bias_ref[pl.ds(h * nblk + blk_b, 1), :] + penalty)
            s = buf[h] + jnp.concatenate([jnp.broadcast_to(row_a, (blk, blk)),
                                          jnp.broadcast_to(row_b, (blk, blk))], axis=0)
            m_new = jnp.maximum(m, jnp.max(s, axis=0, keepdims=True))
            p = jnp.exp2(s - m_new)
            vt = jnp.concatenate([vt_ref[blk_a, hl(h), :], vt_ref[blk_b, hl(h), :]], axis=1)
            new += [m_new, jnp.exp2(m - m_new) * acc + _mm(vt, p)]
        return tuple(new)

    def body(t, carry):
        produce(sb_ref, 2 * t + 1)
        carry = consume(sa_ref, 2 * t, carry)
        produce(sa_ref, 2 * t + 2)
        return consume(sb_ref, 2 * t + 1, carry)

    init = (jnp.full((1, blk), NEG_BIG, F32), jnp.zeros((LANES, blk), F32)) * 2
    carry = lax.fori_loop(0, (npair + 2) // 2, body, init)
    o_t = jnp.concatenate(
        [carry[2 * h + 1][0:HEAD_DIM] / carry[2 * h + 1][HEAD_DIM:HEAD_DIM + 1] for h in heads],
        axis=0)
    o_ref[...] = o_t.T


def _moba_attn(qr, kr, vt, bias, batch, seq):
    t = qr.shape[0]
    nblk = seq // MOBA_BLOCK
    pairs = MOBA_HEADS // 2
    return pl.pallas_call(
        functools.partial(_moba_attn_kernel, nblk=nblk),
        grid=(batch, pairs, nblk),
        in_specs=[pl.BlockSpec((MOBA_BLOCK, LANES), lambda b, p, i: (b * nblk + i, p)),
                  pl.BlockSpec((seq, LANES), lambda b, p, i: (b, p)),
                  pl.BlockSpec((None, nblk, 2 * LANES, MOBA_BLOCK), lambda b, p, i: (b, 0, p, 0)),
                  pl.BlockSpec((None, None, 2 * nblk, MOBA_BLOCK), lambda b, p, i: (b, i, p, 0))],
        out_specs=pl.BlockSpec((MOBA_BLOCK, LANES), lambda b, p, i: (b * nblk + i, p)),
        out_shape=jax.ShapeDtypeStruct((t, MOBA_W), F32),
        scratch_shapes=[pltpu.VMEM((2, 2 * MOBA_BLOCK, MOBA_BLOCK), F32),
                        pltpu.VMEM((2, 2 * MOBA_BLOCK, MOBA_BLOCK), F32)],
        compiler_params=_params(("arbitrary", "arbitrary", "arbitrary")),
        name="moba_attn",
    )(qr, kr, vt, bias)


def _outproj_kernel(og_ref, od_ref, om_ref, x_ref, w1_ref, w2_ref, w3_ref, nw_ref, o_ref):
    mix = (_mm(og_ref[...], w1_ref[...]) + _mm(od_ref[...], w2_ref[...])
           + _mm(om_ref[...], w3_ref[...]))
    ms = jnp.mean(mix * mix, axis=-1, keepdims=True)
    o_ref[...] = x_ref[...] + mix * lax.rsqrt(ms + RMS_EPS) * nw_ref[...]


def _outproj(og, od, om, xf, w1, w2, w3, nw, tm):
    t, d = xf.shape
    row = lambda i: (i, 0)
    fixed = lambda i: (0, 0)
    full = lambda a: pl.BlockSpec(a.shape, fixed)
    tile = lambda a: pl.BlockSpec((tm, a.shape[1]), row)
    return pl.pallas_call(
        _outproj_kernel,
        grid=(t // tm,),
        in_specs=[tile(og), tile(od), tile(om), tile(xf), full(w1), full(w2), full(w3), full(nw)],
        out_specs=pl.BlockSpec((tm, d), row),
        out_shape=jax.ShapeDtypeStruct((t, d), F32),
        compiler_params=_params(("arbitrary",)),
        name="outproj",
    )(og, od, om, xf, w1, w2, w3, nw)


def _ffn_kernel(x_ref, npre_ref, wg_ref, wu_ref, wd_ref, npost_ref, o_ref, h_ref, acc_ref):
    f = pl.program_id(1)

    @pl.when(f == 0)
    def _():
        x = x_ref[...]
        ms = jnp.mean(x * x, axis=-1, keepdims=True)
        h_ref[...] = (x * lax.rsqrt(ms + RMS_EPS) * npre_ref[...]).astype(h_ref.dtype)
        acc_ref[...] = jnp.zeros_like(acc_ref)

    h = h_ref[...]
    g = jnp.dot(h, wg_ref[...], preferred_element_type=F32)
    u = jnp.dot(h, wu_ref[...], preferred_element_type=F32)
    acc_ref[...] += _mm(_silu(g) * u, wd_ref[...])

    @pl.when(f == pl.num_programs(1) - 1)
    def _():
        y = acc_ref[...]
        ms = jnp.mean(y * y, axis=-1, keepdims=True)
        o_ref[...] = x_ref[...] + y * lax.rsqrt(ms + RMS_EPS) * npost_ref[...]


def _ffn(xf, npre, wg, wu, wd, npost, tm, tf):
    t, d = xf.shape
    dff = wg.shape[1]
    return pl.pallas_call(
        _ffn_kernel,
        grid=(t // tm, dff // tf),
        in_specs=[pl.BlockSpec((tm, d), lambda i, f: (i, 0)),
                  pl.BlockSpec((1, d), lambda i, f: (0, 0)),
                  pl.BlockSpec((d, tf), lambda i, f: (0, f)),
                  pl.BlockSpec((d, tf), lambda i, f: (0, f)),
                  pl.BlockSpec((tf, d), lambda i, f: (f, 0)),
                  pl.BlockSpec((1, d), lambda i, f: (0, 0))],
        out_specs=pl.BlockSpec((tm, d), lambda i, f: (i, 0)),
        out_shape=jax.ShapeDtypeStruct((t, d), F32),
        scratch_shapes=[pltpu.VMEM((tm, d), MXU_DTYPE), pltpu.VMEM((tm, d), F32)],
        compiler_params=_params(("arbitrary", "arbitrary")),
        name="ffn",
    )(xf, npre, wg, wu, wd, npost)


def _tile_sizes(tokens, seq):
    pick = lambda want, n: want if n % want == 0 else MOBA_BLOCK
    return dict(proj=pick(512, tokens), ffn=pick(1024, tokens), rec=pick(256, seq), ffn_cols=256)


def _rope_tables(seq):
    half = HEAD_DIM // 2
    inv = ROPE_THETA ** (-jnp.arange(half, dtype=F32) / half)
    ang = jnp.arange(seq).astype(F32)[:, None] * inv[None, :]
    cos, sin = jnp.cos(ang), jnp.sin(ang)
    reps = LANES // HEAD_DIM
    return (jnp.tile(jnp.concatenate([cos, cos], axis=1), (1, reps)),
            jnp.tile(jnp.concatenate([-sin, sin], axis=1), (1, reps)))


def _head_indicator(first_lane, heads):
    ind = np.zeros((LANES, heads * HEAD_DIM), np.float32)
    for h in range(heads):
        ind[first_lane + h, h * HEAD_DIM:(h + 1) * HEAD_DIM] = 1.0
    return jnp.asarray(ind, MXU_DTYPE)


def _lane_row(values, first_lane):
    return jnp.zeros((1, LANES), F32).at[0, first_lane:first_lane + values.shape[0]].set(
        values.astype(F32))


def kernel(x, norm_mix_pre, norm_mix_post, norm_ffn_pre, norm_ffn_post, w_in, w_o, gla_w_gate,
           gla_b_gate, gla_norm, gdn_conv, gdn_a_log, gdn_dt_bias, gdn_norm, ffn_w_gate, ffn_w_up,
           ffn_w_down):
    batch, seq, d_model = x.shape
    depth = w_in.shape[0]
    tokens = batch * seq
    assert seq % (2 * MOBA_BLOCK) == 0
    ts = _tile_sizes(tokens, seq)

    cos, sin = _rope_tables(seq)
    tri = jnp.asarray(np.tril(np.ones((CHUNK, CHUNK), np.float32)), MXU_DTYPE)
    indb = _head_indicator(0, GDN_HEADS)
    indg = _head_indicator(GDN_HEADS, GDN_HEADS)
    row2d = lambda a: a.reshape(1, -1).astype(F32)
    pad_cols = lambda a, n: jnp.pad(a, ((0, 0), (0, n - a.shape[1])))
    bf = lambda a: a.astype(MXU_DTYPE)

    o_ga = 4 * GLA_W
    o_d = o_ga + GLA_GATE_RANK
    o_db = o_d + 4 * GDN_W
    o_m = o_db + 2 * GDN_HEADS

    xf = x.reshape(tokens, d_model)
    for l in range(depth):
        wl = w_in[l]
        w_gla = bf(jnp.concatenate([wl[:, :o_ga], pad_cols(wl[:, o_ga:o_d], LANES)], axis=1))
        w_gdn = bf(jnp.concatenate([wl[:, o_d:o_db], pad_cols(wl[:, o_db:o_m], LANES)], axis=1))
        w_moba = bf(wl[:, o_m:])
        pg, pd, pm = _inproj(xf, row2d(norm_mix_pre[l]), w_gla, w_gdn, w_moba, ts["proj"])

        wgate = bf(jnp.pad(gla_w_gate[l], ((0, LANES - GLA_GATE_RANK), (0, 0))))
        o_gla = _gla(pg, wgate, row2d(gla_b_gate[l]),
                     row2d(jnp.tile(gla_norm[l], LANES // HEAD_DIM)), tri, batch, seq, ts["rec"])

        conv = gdn_conv[l].astype(F32)
        o_gdn = _gdn(pd, conv[:, :GDN_W], conv[:, GDN_W:2 * GDN_W], conv[:, 2 * GDN_W:],
                     _lane_row(gdn_a_log[l], GDN_HEADS), _lane_row(gdn_dt_bias[l], GDN_HEADS),
                     row2d(jnp.tile(gdn_norm[l], GDN_HEADS)), tri, indb, indg,
                     batch, seq, ts["rec"])

        o_moba = _moba_attn(*_moba_prep(pm, cos, sin, batch, seq), batch, seq)

        wo = bf(w_o[l])
        xf = _outproj(o_gla, o_gdn, o_moba, xf, wo[:GLA_W], wo[GLA_W:GLA_W + GDN_W],
                      wo[GLA_W + GDN_W:], row2d(norm_mix_post[l]), ts["proj"])
        xf = _ffn(xf, row2d(norm_ffn_pre[l]), bf(ffn_w_gate[l]), bf(ffn_w_up[l]),
                  bf(ffn_w_down[l]), row2d(norm_ffn_post[l]), ts["ffn"], ts["ffn_cols"])
    return xf.reshape(batch, seq, d_model)
```

```python
import functools

import numpy as np
import jax
import jax.numpy as jnp
from jax import lax
from jax.experimental import pallas as pl
from jax.experimental.pallas import tpu as pltpu

F32 = jnp.float32
MXU_DTYPE = jnp.bfloat16

HEAD_DIM = 64
GLA_HEADS = 4
GDN_HEADS = 4
MOBA_HEADS = 8
GLA_W = GLA_HEADS * HEAD_DIM
GDN_W = GDN_HEADS * HEAD_DIM
MOBA_W = MOBA_HEADS * HEAD_DIM
GLA_GATE_RANK = 16
GLA_GATE_TAU = 16.0
CHUNK = 64
CHUNK_LEVELS = 6
CONV_WIDTH = 4
MOBA_BLOCK = 256
MOBA_TOPK = 3
MOBA_VT_ROWS = 80
ROPE_THETA = 10000.0
RMS_EPS = 1e-6
LANES = 128
NEG_BIG = -1e30
LOG2_E = 1.4426950408889634
VMEM_LIMIT = 48 * 1024 * 1024


def _mm(a, b):
    return jnp.dot(a.astype(MXU_DTYPE), b.astype(MXU_DTYPE), preferred_element_type=F32)


def _mm_nt(a, b):
    return lax.dot_general(a.astype(MXU_DTYPE), b.astype(MXU_DTYPE),
                           (((1,), (1,)), ((), ())), preferred_element_type=F32)


def _mm_tn(a, b):
    return lax.dot_general(a.astype(MXU_DTYPE), b.astype(MXU_DTYPE),
                           (((0,), (0,)), ((), ())), preferred_element_type=F32)


def _split2(x):
    hi = x.astype(MXU_DTYPE)
    lo = (x - hi.astype(F32)).astype(MXU_DTYPE)
    return hi, lo


def _split3(x):
    hi = x.astype(MXU_DTYPE)
    r = x - hi.astype(F32)
    mid = r.astype(MXU_DTYPE)
    lo = (r - mid.astype(F32)).astype(MXU_DTYPE)
    return hi, mid, lo


def _mm_exact_rhs(c, x):
    hi, lo = _split2(x)
    return _mm(c, hi) + _mm(c, lo)


def _mm_exact_lhs3(x, c):
    hi, mid, lo = _split3(x)
    return _mm(hi, c) + _mm(mid, c) + _mm(lo, c)


def _sigmoid(x):
    return 1.0 / (1.0 + jnp.exp(-x))


def _softplus(x):
    return jnp.maximum(x, 0.0) + jnp.log(1.0 + jnp.exp(-jnp.abs(x)))


def _silu(x):
    return x * _sigmoid(x)


def _tile_rows(x, n):
    return jnp.concatenate([x] * n, axis=0)


def _block_diag_mask(rows, cols, rblk, cblk):
    r = lax.broadcasted_iota(jnp.int32, (rows, cols), 0) // rblk
    c = lax.broadcasted_iota(jnp.int32, (rows, cols), 1) // cblk
    return r == c


def _params(sem):
    return pltpu.CompilerParams(dimension_semantics=sem, vmem_limit_bytes=VMEM_LIMIT)


def _inproj_kernel(x_ref, nw_ref, w1_ref, w2_ref, w3_ref, o1_ref, o2_ref, o3_ref):
    x = x_ref[...]
    ms = jnp.mean(x * x, axis=-1, keepdims=True)
    h = (x * lax.rsqrt(ms + RMS_EPS) * nw_ref[...]).astype(MXU_DTYPE)
    o1_ref[...] = jnp.dot(h, w1_ref[...], preferred_element_type=F32)
    o2_ref[...] = jnp.dot(h, w2_ref[...], preferred_element_type=F32)
    o3_ref[...] = jnp.dot(h, w3_ref[...], preferred_element_type=F32)


def _inproj(xf, nw, w1, w2, w3, tm):
    t, d = xf.shape
    n1, n2, n3 = w1.shape[1], w2.shape[1], w3.shape[1]
    row = lambda i: (i, 0)
    fixed = lambda i: (0, 0)
    return pl.pallas_call(
        _inproj_kernel,
        grid=(t // tm,),
        in_specs=[pl.BlockSpec((tm, d), row), pl.BlockSpec((1, d), fixed),
                  pl.BlockSpec((d, n1), fixed), pl.BlockSpec((d, n2), fixed),
                  pl.BlockSpec((d, n3), fixed)],
        out_specs=[pl.BlockSpec((tm, n1), row), pl.BlockSpec((tm, n2), row),
                   pl.BlockSpec((tm, n3), row)],
        out_shape=[jax.ShapeDtypeStruct((t, n1), F32), jax.ShapeDtypeStruct((t, n2), F32),
                   jax.ShapeDtypeStruct((t, n3), F32)],
        compiler_params=_params(("arbitrary",)),
        name="inproj",
    )(xf, nw, w1, w2, w3)


def _gla_level_decays(g, cum, ridx):
    prev = pltpu.roll(g, 1, 0)
    nxt = pltpu.roll(g, CHUNK - 1, 0)
    r4 = ridx % 4
    logs = [jnp.where(ridx % 2 == 1, g, 0.0),
            jnp.where(r4 == 0, nxt, jnp.where(r4 == 1, 0.0, jnp.where(r4 == 2, g, g + prev)))]
    for lvl in range(2, CHUNK_LEVELS):
        half = 1 << lvl
        ref = jnp.concatenate(
            [jnp.broadcast_to(cum[b + half - 1:b + half, :], (2 * half, LANES))
             for b in range(0, CHUNK, 2 * half)], axis=0)
        logs.append(-jnp.abs(cum - ref))
    return [jnp.exp(d) for d in logs]


def _gla_kernel(q_ref, k_ref, v_ref, z_ref, a_ref, wg_ref, bg_ref, nw_ref, tri_ref,
                o_ref, st_ref, *, rows):
    @pl.when(pl.program_id(2) == 0)
    def _():
        st_ref[...] = jnp.zeros_like(st_ref)

    pre = _mm(a_ref[...], wg_ref[...]) + bg_ref[...]
    log_a = -_softplus(-pre) * (1.0 / GLA_GATE_TAU)

    lane = lax.broadcasted_iota(jnp.int32, (CHUNK, LANES), 1)
    head0 = lane < HEAD_DIM
    ridx = lax.broadcasted_iota(jnp.int32, (CHUNK, LANES), 0)
    head_ones = _block_diag_mask(LANES, LANES, HEAD_DIM, HEAD_DIM)
    head_ones_f = head_ones.astype(MXU_DTYPE)
    ri2 = lax.broadcasted_iota(jnp.int32, (2 * CHUNK, CHUNK), 0) % CHUNK
    ci2 = lax.broadcasted_iota(jnp.int32, (2 * CHUNK, CHUNK), 1)
    same_blk = [(ri2 >> (l + 1)) == (ci2 >> (l + 1)) for l in range(CHUNK_LEVELS)]
    upper = [((ridx >> l) & 1) == 1 for l in range(CHUNK_LEVELS)]
    tri = tri_ref[...]

    st = st_ref[...]
    outs = []
    for c in range(rows // CHUNK):
        sl = slice(c * CHUNK, (c + 1) * CHUNK)
        q = q_ref[sl, :] * (HEAD_DIM ** -0.5)
        k = k_ref[sl, :]
        v = v_ref[sl, :]
        g = log_a[sl, :]
        cum = _mm_exact_rhs(tri, g)
        last = cum[CHUNK - 1:CHUNK, :]
        q_dec = q * jnp.exp(cum)
        k_dec = k * jnp.exp(last - cum)
        e_last = jnp.exp(last)
        decays = _gla_level_decays(g, cum, ridx)

        a_st = jnp.zeros((2 * CHUNK, CHUNK), F32)
        for l in range(CHUNK_LEVELS):
            e = decays[l]
            ql = jnp.where(upper[l], q * e, 0.0)
            kl = jnp.where(upper[l], 0.0, k * e)
            lhs = jnp.concatenate([jnp.where(head0, ql, 0.0), jnp.where(head0, 0.0, ql)], axis=0)
            a_st = a_st + jnp.where(same_blk[l], _mm_nt(lhs, kl), 0.0)
        o_full = _mm(a_st, v)
        o_intra = jnp.where(head0, o_full[0:CHUNK], o_full[CHUNK:2 * CHUNK])
        diag = _mm(q * k, head_ones_f)
        outs.append(o_intra + diag * v + _mm_nt(q_dec, st))
        st = st * e_last + jnp.where(head_ones, _mm_tn(v, k_dec), 0.0)
    st_ref[...] = st

    o_all = jnp.concatenate(outs, axis=0)
    ms = _mm(o_all * o_all, head_ones_f) * (1.0 / HEAD_DIM)
    y = o_all * lax.rsqrt(ms + RMS_EPS) * nw_ref[...]
    o_ref[...] = y * _silu(z_ref[...])


def _gla(pg, wg, bg, nw, tri, batch, seq, rows):
    t = pg.shape[0]
    nt = seq // rows
    pairs = GLA_W // LANES
    col = lambda off: (lambda b, p, i: (b * nt + i, off + p))
    fixed = lambda b, p, i: (0, 0)
    blk = lambda off: pl.BlockSpec((rows, LANES), col(off))
    return pl.pallas_call(
        functools.partial(_gla_kernel, rows=rows),
        grid=(batch, pairs, nt),
        in_specs=[blk(0), blk(pairs), blk(2 * pairs), blk(3 * pairs),
                  pl.BlockSpec((rows, LANES), lambda b, p, i: (b * nt + i, 4 * pairs)),
                  pl.BlockSpec((LANES, LANES), lambda b, p, i: (0, p)),
                  pl.BlockSpec((1, LANES), lambda b, p, i: (0, p)),
                  pl.BlockSpec((1, LANES), fixed),
                  pl.BlockSpec(tri.shape, fixed)],
        out_specs=pl.BlockSpec((rows, LANES), lambda b, p, i: (b * nt + i, p)),
        out_shape=jax.ShapeDtypeStruct((t, GLA_W), F32),
        scratch_shapes=[pltpu.VMEM((LANES, LANES), F32)],
        compiler_params=_params(("arbitrary", "arbitrary", "arbitrary")),
        name="gla",
    )(pg, pg, pg, pg, pg, wg, bg, nw, tri)


def _gdn_kernel(q_ref, k_ref, v_ref, z_ref, e_ref, cq_ref, ck_ref, cv_ref, alog_ref, dt_ref,
                nw_ref, tri_ref, indb_ref, indg_ref, o_ref, bq_ref, bk_ref, bv_ref, st_ref,
                *, rows):
    w = GDN_W
    halo = 8

    @pl.when(pl.program_id(1) == 0)
    def _():
        st_ref[...] = jnp.zeros_like(st_ref)
        for buf in (bq_ref, bk_ref, bv_ref):
            buf[0:halo, :] = jnp.zeros((halo, w), F32)

    def conv_silu(x_ref, buf, cw_ref):
        buf[halo:halo + rows, :] = x_ref[...]
        acc = jnp.zeros((rows, w), F32)
        for i in range(CONV_WIDTH):
            acc = acc + cw_ref[i:i + 1, :] * buf[pl.ds(halo - (CONV_WIDTH - 1) + i, rows), :]
        buf[0:halo, :] = buf[rows:rows + halo, :]
        return _silu(acc)

    q = conv_silu(q_ref, bq_ref, cq_ref)
    k = conv_silu(k_ref, bk_ref, ck_ref)
    v = conv_silu(v_ref, bv_ref, cv_ref)

    head_ones = _block_diag_mask(w, w, HEAD_DIM, HEAD_DIM)
    head_ones_f = head_ones.astype(MXU_DTYPE)

    sumsq = lambda x: _mm(x * x, head_ones_f)
    q = q * lax.rsqrt(sumsq(q) + RMS_EPS) * (HEAD_DIM ** -0.5)
    k = k * lax.rsqrt(sumsq(k) + RMS_EPS)

    extra = e_ref[...]
    beta_e = _mm_exact_lhs3(_sigmoid(extra), indb_ref[...])
    g_small = -jnp.exp(alog_ref[...]) * _softplus(extra + dt_ref[...])

    lane = lax.broadcasted_iota(jnp.int32, (CHUNK, w), 1) % HEAD_DIM
    ridx = lax.broadcasted_iota(jnp.int32, (CHUNK, w), 0)
    incl = lane <= ridx
    strict = lane < ridx
    tri = tri_ref[...]
    indg = indg_ref[...]

    bd = lambda x: jnp.where(head_ones, _tile_rows(x, GDN_HEADS), 0.0)
    eye_all = (lane == ridx).astype(F32)
    nchunk = rows // CHUNK

    xs, ps, keep = [], [], []
    for c in range(nchunk):
        sl = slice(c * CHUNK, (c + 1) * CHUNK)
        qc, kc, vc, bc = q[sl], k[sl], v[sl], beta_e[sl]
        h3 = _split3(g_small[sl])
        gc_small = _mm(tri, h3[0]) + _mm(tri, h3[1]) + _mm(tri, h3[2])
        gc = _mm_exact_lhs3(gc_small, indg)
        eg = jnp.exp(gc)
        g_last = gc[CHUNK - 1:CHUNK, :]
        kb = kc * bc

        g_hi, g_mid, g_lo = [p.astype(F32) for p in _split3(gc)]
        one = jnp.ones_like(gc)
        zero = jnp.zeros_like(gc)
        a2 = jnp.where(lane == 0, g_hi, jnp.where(lane == 1, g_mid, jnp.where(
            lane == 2, g_lo, jnp.where(lane < 6, one, zero))))
        b2 = jnp.where(lane < 3, one, jnp.where(lane == 3, -g_hi, jnp.where(
            lane == 4, -g_mid, jnp.where(lane == 5, -g_lo, zero))))
        decay = jnp.exp(jnp.where(incl, _mm_nt(a2, bd(b2)), NEG_BIG))

        kk = _mm_nt(jnp.concatenate([kb, qc], axis=0), bd(kc))
        xs.append(-jnp.where(strict, kk[0:CHUNK] * decay, 0.0))
        ps.append(eye_all)
        keep.append(dict(attn=kk[CHUNK:2 * CHUNK] * decay, vb=vc * bc, kbg=kb * eg, qg=qc * eg,
                         k_dec=kc * jnp.exp(g_last - gc), e_last=jnp.exp(g_last)))

    for lvl in range(CHUNK_LEVELS):
        for c in range(nchunk):
            rhs = bd(xs[c])
            if lvl < CHUNK_LEVELS - 1:
                r = _mm(jnp.concatenate([xs[c], ps[c]], axis=0), rhs)
                xs[c] = r[0:CHUNK]
                ps[c] = ps[c] + r[CHUNK:2 * CHUNK]
            else:
                ps[c] = ps[c] + _mm(ps[c], rhs)

    steps = []
    for c in range(nchunk):
        kp = keep[c]
        uw = _mm(ps[c], jnp.concatenate([bd(kp["vb"]), bd(kp["kbg"])], axis=1))
        u, wmat = uw[:, 0:w], uw[:, w:2 * w]
        ao = _mm(kp["attn"], jnp.concatenate([bd(wmat), bd(u)], axis=1))
        nb = _mm_tn(jnp.concatenate([wmat, u], axis=1), kp["k_dec"])
        steps.append((kp["qg"] - ao[:, 0:w], ao[:, w:2 * w], kp["e_last"],
                      jnp.where(head_ones, nb[0:w], 0.0), jnp.where(head_ones, nb[w:2 * w], 0.0)))

    st = st_ref[...]
    outs = []
    for q_eff, o_const, e_last, n_mat, b_mat in steps:
        outs.append(_mm_nt(q_eff, st) + o_const)
        st = st * e_last - _mm(st, n_mat) + b_mat
    st_ref[...] = st

    o_all = jnp.concatenate(outs, axis=0)
    ms = _mm(o_all * o_all, head_ones_f) * (1.0 / HEAD_DIM)
    y = o_all * lax.rsqrt(ms + RMS_EPS) * nw_ref[...]
    o_ref[...] = y * _silu(z_ref[...])


def _gdn(pd, cq, ck, cv, alog, dt, nw, tri, indb, indg, batch, seq, rows):
    t = pd.shape[0]
    nt = seq // rows
    w = GDN_W
    fixed = lambda b, i: (0, 0)
    blk = lambda off: pl.BlockSpec((rows, w), lambda b, i: (b * nt + i, off))
    small = lambda a: pl.BlockSpec(a.shape, fixed)
    return pl.pallas_call(
        functools.partial(_gdn_kernel, rows=rows),
        grid=(batch, nt),
        in_specs=[blk(0), blk(1), blk(2), blk(3),
                  pl.BlockSpec((rows, LANES), lambda b, i: (b * nt + i, 4 * w // LANES)),
                  small(cq), small(ck), small(cv), small(alog), small(dt), small(nw),
                  small(tri), small(indb), small(indg)],
        out_specs=pl.BlockSpec((rows, w), lambda b, i: (b * nt + i, 0)),
        out_shape=jax.ShapeDtypeStruct((t, w), F32),
        scratch_shapes=[pltpu.VMEM((rows + 8, w), F32), pltpu.VMEM((rows + 8, w), F32),
                        pltpu.VMEM((rows + 8, w), F32), pltpu.VMEM((w, w), F32)],
        compiler_params=_params(("arbitrary", "arbitrary")),
        name="gdn",
    )(pd, pd, pd, pd, pd, cq, ck, cv, alog, dt, nw, tri, indb, indg)


def _moba_prep_kernel(q_ref, k_ref, v_ref, cos_ref, sin_ref,
                      qo_ref, ko_ref, vt_ref, bias_ref, km_ref, *, nblk):
    n = pl.program_id(1)
    blk = MOBA_BLOCK
    hw = MOBA_W

    @pl.when(n == 0)
    def _():
        km_ref[...] = jnp.zeros_like(km_ref)

    cosf = cos_ref[...]
    sinf = sin_ref[...]
    lane = lax.broadcasted_iota(jnp.int32, (blk, LANES), 1)
    first_half = (lane % HEAD_DIM) < (HEAD_DIM // 2)

    def rope(x_ref):
        parts = []
        for c in range(hw // LANES):
            x = x_ref[:, c * LANES:(c + 1) * LANES]
            swapped = jnp.where(first_half, pltpu.roll(x, LANES - HEAD_DIM // 2, 1),
                                pltpu.roll(x, HEAD_DIM // 2, 1))
            parts.append(x * cosf + swapped * sinf)
        return jnp.concatenate(parts, axis=1)

    q = rope(q_ref) * (HEAD_DIM ** -0.5 * LOG2_E)
    k = rope(k_ref)

    km_rows = jnp.where(_block_diag_mask(MOBA_HEADS * nblk, hw, nblk, HEAD_DIM),
                        _tile_rows(km_ref[...], MOBA_HEADS), 0.0)
    km_hi, km_lo = _split2(km_rows)
    q_hi, q_lo = _split2(q)
    gate = _mm_nt(km_hi, q_hi) + _mm_nt(km_hi, q_lo) + _mm_nt(km_lo, q_hi)
    gate = gate.reshape(MOBA_HEADS, nblk, blk)
    kb = lax.broadcasted_iota(jnp.int32, (MOBA_HEADS, nblk, blk), 1)
    past = kb < n
    g = jnp.where(past, gate, -jnp.inf)
    sel = jnp.zeros((MOBA_HEADS, nblk, blk), jnp.bool_)
    for _ in range(MOBA_TOPK):
        m = jnp.max(g, axis=1, keepdims=True)
        first = jnp.min(jnp.where(g == m, kb, nblk), axis=1, keepdims=True)
        pick = kb == first
        sel = sel | (pick & past)
        g = jnp.where(pick, -jnp.inf, g)
    bias_ref[...] = jnp.where(sel, 0.0, NEG_BIG).reshape(MOBA_HEADS * nblk, blk)

    qo_ref[...] = q.astype(qo_ref.dtype)
    ko_ref[...] = k.astype(ko_ref.dtype)
    v_t = v_ref[...].T
    ones_row = (lax.broadcasted_iota(jnp.int32, (MOBA_VT_ROWS - HEAD_DIM, blk), 0) == 0).astype(F32)
    vt_ref[...] = jnp.concatenate(
        [part for h in range(MOBA_HEADS)
         for part in (v_t[h * HEAD_DIM:(h + 1) * HEAD_DIM], ones_row)], axis=0).astype(vt_ref.dtype)
    km_ref[pl.ds(n, 1), :] = jnp.mean(k, axis=0, keepdims=True)


def _moba_prep(pm, cos, sin, batch, seq):
    t = pm.shape[0]
    nblk = seq // MOBA_BLOCK
    hw = MOBA_W
    blk = lambda off: pl.BlockSpec((MOBA_BLOCK, hw), lambda b, i: (b * nblk + i, off))
    tab = pl.BlockSpec((MOBA_BLOCK, LANES), lambda b, i: (i, 0))
    row_out = pl.BlockSpec((MOBA_BLOCK, hw), lambda b, i: (b * nblk + i, 0))
    per_block = lambda r: pl.BlockSpec((None, None, r, MOBA_BLOCK), lambda b, i: (b, i, 0, 0))
    return pl.pallas_call(
        functools.partial(_moba_prep_kernel, nblk=nblk),
        grid=(batch, nblk),
        in_specs=[blk(0), blk(1), blk(2), tab, tab],
        out_specs=[row_out, row_out, per_block(MOBA_HEADS * MOBA_VT_ROWS),
                   per_block(MOBA_HEADS * nblk)],
        out_shape=[jax.ShapeDtypeStruct((t, hw), MXU_DTYPE),
                   jax.ShapeDtypeStruct((t, hw), MXU_DTYPE),
                   jax.ShapeDtypeStruct((batch, nblk, MOBA_HEADS * MOBA_VT_ROWS, MOBA_BLOCK),
                                        MXU_DTYPE),
                   jax.ShapeDtypeStruct((batch, nblk, MOBA_HEADS * nblk, MOBA_BLOCK), F32)],
        scratch_shapes=[pltpu.VMEM((nblk, hw), F32)],
        compiler_params=_params(("arbitrary", "arbitrary")),
        name="moba_prep",
    )(pm, pm, pm, cos, sin)


def _moba_attn_kernel(q_ref, k_ref, vt_ref, bias_ref, o_ref, sa_ref, sb_ref, *, nblk):
    m_step = pl.program_id(2)
    blk = MOBA_BLOCK
    two = 2 * blk
    heads = range(2)
    key = lax.broadcasted_iota(jnp.int32, (two, two), 0)
    qry = lax.broadcasted_iota(jnp.int32, (two, two), 1)
    lane = lax.broadcasted_iota(jnp.int32, (two, LANES), 1)
    hv = lambda h: slice(h * MOBA_VT_ROWS, (h + 1) * MOBA_VT_ROWS)
    q_pair = q_ref[...]
    q_head = [jnp.where((lane // HEAD_DIM) == h, q_pair, jnp.zeros_like(q_pair)) for h in heads]
    zero_row = jnp.zeros((1, blk), F32)

    def produce(buf, pair):
        k_pair = k_ref[pl.ds(pl.multiple_of(pair * two, two), two), :]
        for h in heads:
            buf[h] = _mm_nt(k_pair, q_head[h])

    def bias_rows(h, item):
        if item is None:
            blk_a = 2 * m_step
            row_a = jnp.concatenate([zero_row, bias_ref[1, pl.ds(h * nblk + blk_a, 1), :]], axis=1)
            return blk_a, blk_a + 1, row_a, jnp.zeros((1, two), F32)
        blk_a = 2 * (item - 1)
        rows = [jnp.concatenate([bias_ref[0, pl.ds(h * nblk + j, 1), :],
                                 bias_ref[1, pl.ds(h * nblk + j, 1), :]], axis=1)
                for j in (blk_a, blk_a + 1)]
        return blk_a, blk_a + 1, rows[0], rows[1]

    def consume(buf, item, carry):
        new = []
        for h in heads:
            m, acc = carry[2 * h], carry[2 * h + 1]
            blk_a, blk_b, row_a, row_b = bias_rows(h, item)
            s = buf[h]
            if item is None:
                s = jnp.where(key <= qry, s, NEG_BIG)
            s_a, s_b = s[0:blk], s[blk:two]
            m_new = jnp.maximum(m, jnp.maximum(jnp.max(s_a, axis=0, keepdims=True) + row_a,
                                               jnp.max(s_b, axis=0, keepdims=True) + row_b))
            p = jnp.concatenate([jnp.exp2(s_a - (m_new - row_a)),
                                 jnp.exp2(s_b - (m_new - row_b))], axis=0)
            vt = jnp.concatenate([vt_ref[blk_a, hv(h), :], vt_ref[blk_b, hv(h), :]], axis=1)
            new += [m_new, jnp.exp2(m - m_new) * acc + _mm(vt, p)]
        return tuple(new)

    init = (jnp.full((1, two), NEG_BIG, F32), jnp.zeros((MOBA_VT_ROWS, two), F32)) * 2
    produce(sa_ref, m_step)
    produce(sb_ref, 0)
    carry = consume(sa_ref, None, init)

    def body(t, carry):
        produce(sa_ref, 2 * t + 1)
        carry = consume(sb_ref, 2 * t + 1, carry)
        produce(sb_ref, 2 * t + 2)
        return consume(sa_ref, 2 * t + 2, carry)

    carry = lax.fori_loop(0, m_step // 2, body, carry)
    carry = lax.cond(m_step % 2 == 1, lambda c: consume(sb_ref, m_step, c), lambda c: c, carry)
    o_t = jnp.concatenate(
        [carry[2 * h + 1][0:HEAD_DIM] / carry[2 * h + 1][HEAD_DIM:HEAD_DIM + 1] for h in heads],
        axis=0)
    o_ref[...] = o_t.T


def _moba_attn(qr, kr, vt, bias, batch, seq):
    t = qr.shape[0]
    nblk = seq // MOBA_BLOCK
    nstep = nblk // 2
    pairs = MOBA_HEADS // 2
    two = 2 * MOBA_BLOCK
    return pl.pallas_call(
        functools.partial(_moba_attn_kernel, nblk=nblk),
        grid=(batch, pairs, nstep),
        in_specs=[pl.BlockSpec((two, LANES), lambda b, p, i: (b * nstep + i, p)),
                  pl.BlockSpec((seq, LANES), lambda b, p, i: (b, p)),
                  pl.BlockSpec((None, nblk, 2 * MOBA_VT_ROWS, MOBA_BLOCK),
                               lambda b, p, i: (b, 0, p, 0)),
                  pl.BlockSpec((None, 2, 2 * nblk, MOBA_BLOCK), lambda b, p, i: (b, i, p, 0))],
        out_specs=pl.BlockSpec((two, LANES), lambda b, p, i: (b * nstep + i, p)),
        out_shape=jax.ShapeDtypeStruct((t, MOBA_W), F32),
        scratch_shapes=[pltpu.VMEM((2, two, two), F32), pltpu.VMEM((2, two, two), F32)],
        compiler_params=_params(("arbitrary", "arbitrary", "arbitrary")),
        name="moba_attn",
    )(qr, kr, vt, bias)


def _outproj_kernel(og_ref, od_ref, om_ref, x_ref, w1_ref, w2_ref, w3_ref, nw_ref, o_ref):
    mix = (_mm(og_ref[...], w1_ref[...]) + _mm(od_ref[...], w2_ref[...])
           + _mm(om_ref[...], w3_ref[...]))
    ms = jnp.mean(mix * mix, axis=-1, keepdims=True)
    o_ref[...] = x_ref[...] + mix * lax.rsqrt(ms + RMS_EPS) * nw_ref[...]


def _outproj(og, od, om, xf, w1, w2, w3, nw, tm):
    t, d = xf.shape
    row = lambda i: (i, 0)
    fixed = lambda i: (0, 0)
    full = lambda a: pl.BlockSpec(a.shape, fixed)
    tile = lambda a: pl.BlockSpec((tm, a.shape[1]), row)
    return pl.pallas_call(
        _outproj_kernel,
        grid=(t // tm,),
        in_specs=[tile(og), tile(od), tile(om), tile(xf), full(w1), full(w2), full(w3), full(nw)],
        out_specs=pl.BlockSpec((tm, d), row),
        out_shape=jax.ShapeDtypeStruct((t, d), F32),
        compiler_params=_params(("arbitrary",)),
        name="outproj",
    )(og, od, om, xf, w1, w2, w3, nw)


def _ffn_kernel(x_ref, npre_ref, wg_ref, wu_ref, wd_ref, npost_ref, o_ref, h_ref, acc_ref,
                *, col_chunks):
    f = pl.program_id(1)

    @pl.when(f == 0)
    def _():
        x = x_ref[...]
        ms = jnp.mean(x * x, axis=-1, keepdims=True)
        h_ref[...] = (x * lax.rsqrt(ms + RMS_EPS) * npre_ref[...]).astype(h_ref.dtype)
        acc_ref[...] = jnp.zeros_like(acc_ref)

    h = h_ref[...]
    update = None
    for lo, hi in col_chunks:
        g = jnp.dot(h, wg_ref[:, lo:hi], preferred_element_type=F32)
        u = jnp.dot(h, wu_ref[:, lo:hi], preferred_element_type=F32)
        part = _mm(_silu(g) * u, wd_ref[lo:hi, :])
        update = part if update is None else update + part
    acc_ref[...] += update

    @pl.when(f == pl.num_programs(1) - 1)
    def _():
        y = acc_ref[...]
        ms = jnp.mean(y * y, axis=-1, keepdims=True)
        o_ref[...] = x_ref[...] + y * lax.rsqrt(ms + RMS_EPS) * npost_ref[...]


def _ffn(xf, npre, wg, wu, wd, npost, tm, tf, chunk):
    t, d = xf.shape
    dff = wg.shape[1]
    col_chunks = tuple((lo, min(lo + chunk, tf)) for lo in range(0, tf, chunk))
    return pl.pallas_call(
        functools.partial(_ffn_kernel, col_chunks=col_chunks),
        grid=(t // tm, dff // tf),
        in_specs=[pl.BlockSpec((tm, d), lambda i, f: (i, 0)),
                  pl.BlockSpec((1, d), lambda i, f: (0, 0)),
                  pl.BlockSpec((d, tf), lambda i, f: (0, f)),
                  pl.BlockSpec((d, tf), lambda i, f: (0, f)),
                  pl.BlockSpec((tf, d), lambda i, f: (f, 0)),
                  pl.BlockSpec((1, d), lambda i, f: (0, 0))],
        out_specs=pl.BlockSpec((tm, d), lambda i, f: (i, 0)),
        out_shape=jax.ShapeDtypeStruct((t, d), F32),
        scratch_shapes=[pltpu.VMEM((tm, d), MXU_DTYPE), pltpu.VMEM((tm, d), F32)],
        compiler_params=_params(("arbitrary", "arbitrary")),
        name="ffn",
    )(xf, npre, wg, wu, wd, npost)


def _tile_sizes(tokens, seq):
    pick = lambda want, n: want if n % want == 0 else MOBA_BLOCK
    return dict(proj=pick(512, tokens), ffn=pick(512, tokens), rec=pick(512, seq))


def _ffn_cols(d_ff):
    return d_ff // 2 if d_ff % (2 * LANES) == 0 else d_ff


def _rope_tables(seq):
    half = HEAD_DIM // 2
    inv = ROPE_THETA ** (-jnp.arange(half, dtype=F32) / half)
    ang = jnp.arange(seq).astype(F32)[:, None] * inv[None, :]
    cos, sin = jnp.cos(ang), jnp.sin(ang)
    reps = LANES // HEAD_DIM
    return (jnp.tile(jnp.concatenate([cos, cos], axis=1), (1, reps)),
            jnp.tile(jnp.concatenate([-sin, sin], axis=1), (1, reps)))


def _head_indicator(first_lane, heads):
    ind = np.zeros((LANES, heads * HEAD_DIM), np.float32)
    for h in range(heads):
        ind[first_lane + h, h * HEAD_DIM:(h + 1) * HEAD_DIM] = 1.0
    return jnp.asarray(ind, MXU_DTYPE)


def _lane_row(values, first_lane):
    return jnp.zeros((1, LANES), F32).at[0, first_lane:first_lane + values.shape[0]].set(
        values.astype(F32))


def kernel(x, norm_mix_pre, norm_mix_post, norm_ffn_pre, norm_ffn_post, w_in, w_o, gla_w_gate,
           gla_b_gate, gla_norm, gdn_conv, gdn_a_log, gdn_dt_bias, gdn_norm, ffn_w_gate, ffn_w_up,
           ffn_w_down):
    batch, seq, d_model = x.shape
    depth = w_in.shape[0]
    tokens = batch * seq
    assert seq % (2 * MOBA_BLOCK) == 0
    ts = _tile_sizes(tokens, seq)

    cos, sin = _rope_tables(seq)
    tri = jnp.asarray(np.tril(np.ones((CHUNK, CHUNK), np.float32)), MXU_DTYPE)
    indb = _head_indicator(0, GDN_HEADS)
    indg = _head_indicator(GDN_HEADS, GDN_HEADS)
    row2d = lambda a: a.reshape(1, -1).astype(F32)
    pad_cols = lambda a, n: jnp.pad(a, ((0, 0), (0, n - a.shape[1])))
    bf = lambda a: a.astype(MXU_DTYPE)

    o_ga = 4 * GLA_W
    o_d = o_ga + GLA_GATE_RANK
    o_db = o_d + 4 * GDN_W
    o_m = o_db + 2 * GDN_HEADS

    xf = x.reshape(tokens, d_model)
    for l in range(depth):
        wl = w_in[l]
        w_gla = bf(jnp.concatenate([wl[:, :o_ga], pad_cols(wl[:, o_ga:o_d], LANES)], axis=1))
        w_gdn = bf(jnp.concatenate([wl[:, o_d:o_db], pad_cols(wl[:, o_db:o_m], LANES)], axis=1))
        w_moba = bf(wl[:, o_m:])
        pg, pd, pm = _inproj(xf, row2d(norm_mix_pre[l]), w_gla, w_gdn, w_moba, ts["proj"])

        wgate = bf(jnp.pad(gla_w_gate[l], ((0, LANES - GLA_GATE_RANK), (0, 0))))
        o_gla = _gla(pg, wgate, row2d(gla_b_gate[l]),
                     row2d(jnp.tile(gla_norm[l], LANES // HEAD_DIM)), tri, batch, seq, ts["rec"])

        conv = gdn_conv[l].astype(F32)
        o_gdn = _gdn(pd, conv[:, :GDN_W], conv[:, GDN_W:2 * GDN_W], conv[:, 2 * GDN_W:],
                     _lane_row(gdn_a_log[l], GDN_HEADS), _lane_row(gdn_dt_bias[l], GDN_HEADS),
                     row2d(jnp.tile(gdn_norm[l], GDN_HEADS)), tri, indb, indg,
                     batch, seq, ts["rec"])

        o_moba = _moba_attn(*_moba_prep(pm, cos, sin, batch, seq), batch, seq)

        wo = bf(w_o[l])
        xf = _outproj(o_gla, o_gdn, o_moba, xf, wo[:GLA_W], wo[GLA_W:GLA_W + GDN_W],
                      wo[GLA_W + GDN_W:], row2d(norm_mix_post[l]), ts["proj"])
        xf = _ffn(xf, row2d(norm_ffn_pre[l]), bf(ffn_w_gate[l]), bf(ffn_w_up[l]),
                  bf(ffn_w_down[l]), row2d(norm_ffn_post[l]), ts["ffn"],
                  _ffn_cols(ffn_w_gate.shape[2]), 4 * LANES)
    return xf.reshape(batch, seq, d_model)
```

```python
import functools

import numpy as np
import jax
import jax.numpy as jnp
from jax import lax
from jax.experimental import pallas as pl
from jax.experimental.pallas import tpu as pltpu

F32 = jnp.float32
MXU_DTYPE = jnp.bfloat16

HEAD_DIM = 64
GLA_HEADS = 4
GDN_HEADS = 4
MOBA_HEADS = 8
GLA_W = GLA_HEADS * HEAD_DIM
GDN_W = GDN_HEADS * HEAD_DIM
MOBA_W = MOBA_HEADS * HEAD_DIM
GLA_GATE_RANK = 16
GLA_GATE_TAU = 16.0
CHUNK = 64
CHUNK_LEVELS = 6
CONV_WIDTH = 4
MOBA_BLOCK = 256
MOBA_TOPK = 3
MOBA_VT_ROWS = 80
ROPE_THETA = 10000.0
RMS_EPS = 1e-6
LANES = 128
NEG_BIG = -1e30
LOG2_E = 1.4426950408889634
VMEM_LIMIT = 48 * 1024 * 1024
FFN_CHUNK = 512


def _mm(a, b):
    return jnp.dot(a.astype(MXU_DTYPE), b.astype(MXU_DTYPE), preferred_element_type=F32)


def _mm_nt(a, b):
    return lax.dot_general(a.astype(MXU_DTYPE), b.astype(MXU_DTYPE),
                           (((1,), (1,)), ((), ())), preferred_element_type=F32)


def _mm_tn(a, b):
    return lax.dot_general(a.astype(MXU_DTYPE), b.astype(MXU_DTYPE),
                           (((0,), (0,)), ((), ())), preferred_element_type=F32)


def _split2(x):
    hi = x.astype(MXU_DTYPE)
    lo = (x - hi.astype(F32)).astype(MXU_DTYPE)
    return hi, lo


def _split3(x):
    hi = x.astype(MXU_DTYPE)
    r = x - hi.astype(F32)
    mid = r.astype(MXU_DTYPE)
    lo = (r - mid.astype(F32)).astype(MXU_DTYPE)
    return hi, mid, lo


def _mm_exact_rhs(c, x):
    hi, lo = _split2(x)
    return _mm(c, hi) + _mm(c, lo)


def _mm_exact_lhs3(x, c):
    hi, mid, lo = _split3(x)
    return _mm(hi, c) + _mm(mid, c) + _mm(lo, c)


def _sigmoid(x):
    return 1.0 / (1.0 + jnp.exp(-x))


def _softplus(x):
    return jnp.maximum(x, 0.0) + jnp.log(1.0 + jnp.exp(-jnp.abs(x)))


def _silu(x):
    return x * _sigmoid(x)


def _tile_rows(x, n):
    return jnp.concatenate([x] * n, axis=0)


def _block_diag_mask(rows, cols, rblk, cblk):
    r = lax.broadcasted_iota(jnp.int32, (rows, cols), 0) // rblk
    c = lax.broadcasted_iota(jnp.int32, (rows, cols), 1) // cblk
    return r == c


def _params(sem):
    return pltpu.CompilerParams(dimension_semantics=sem, vmem_limit_bytes=VMEM_LIMIT)


def _inproj_kernel(x_ref, nw_ref, w1_ref, w2_ref, w3_ref, o1_ref, o2_ref, o3_ref):
    x = x_ref[...]
    ms = jnp.mean(x * x, axis=-1, keepdims=True)
    h = (x * lax.rsqrt(ms + RMS_EPS) * nw_ref[...]).astype(MXU_DTYPE)
    o1_ref[...] = jnp.dot(h, w1_ref[...], preferred_element_type=F32)
    o2_ref[...] = jnp.dot(h, w2_ref[...], preferred_element_type=F32)
    o3_ref[...] = jnp.dot(h, w3_ref[...], preferred_element_type=F32)


def _inproj(xf, nw, w1, w2, w3, tm):
    t, d = xf.shape
    n1, n2, n3 = w1.shape[1], w2.shape[1], w3.shape[1]
    row = lambda i: (i, 0)
    fixed = lambda i: (0, 0)
    return pl.pallas_call(
        _inproj_kernel,
        grid=(t // tm,),
        in_specs=[pl.BlockSpec((tm, d), row), pl.BlockSpec((1, d), fixed),
                  pl.BlockSpec((d, n1), fixed), pl.BlockSpec((d, n2), fixed),
                  pl.BlockSpec((d, n3), fixed)],
        out_specs=[pl.BlockSpec((tm, n1), row), pl.BlockSpec((tm, n2), row),
                   pl.BlockSpec((tm, n3), row)],
        out_shape=[jax.ShapeDtypeStruct((t, n1), F32), jax.ShapeDtypeStruct((t, n2), F32),
                   jax.ShapeDtypeStruct((t, n3), F32)],
        compiler_params=_params(("arbitrary",)),
        name="inproj",
    )(xf, nw, w1, w2, w3)


def _gla_level_decays(g, cum, ridx):
    prev = pltpu.roll(g, 1, 0)
    nxt = pltpu.roll(g, CHUNK - 1, 0)
    r4 = ridx % 4
    logs = [jnp.where(ridx % 2 == 1, g, 0.0),
            jnp.where(r4 == 0, nxt, jnp.where(r4 == 1, 0.0, jnp.where(r4 == 2, g, g + prev)))]
    for lvl in range(2, CHUNK_LEVELS):
        half = 1 << lvl
        ref = jnp.concatenate(
            [jnp.broadcast_to(cum[b + half - 1:b + half, :], (2 * half, LANES))
             for b in range(0, CHUNK, 2 * half)], axis=0)
        logs.append(-jnp.abs(cum - ref))
    return [jnp.exp(d) for d in logs]


def _gla_kernel(q_ref, k_ref, v_ref, z_ref, a_ref, wg_ref, bg_ref, nw_ref, tri_ref,
                o_ref, st_ref, *, rows):
    @pl.when(pl.program_id(2) == 0)
    def _():
        st_ref[...] = jnp.zeros_like(st_ref)

    pre = _mm(a_ref[...], wg_ref[...]) + bg_ref[...]
    log_a = -_softplus(-pre) * (1.0 / GLA_GATE_TAU)

    lane = lax.broadcasted_iota(jnp.int32, (CHUNK, LANES), 1)
    head0 = lane < HEAD_DIM
    ridx = lax.broadcasted_iota(jnp.int32, (CHUNK, LANES), 0)
    head_ones = _block_diag_mask(LANES, LANES, HEAD_DIM, HEAD_DIM)
    head_ones_f = head_ones.astype(MXU_DTYPE)
    ri2 = lax.broadcasted_iota(jnp.int32, (2 * CHUNK, CHUNK), 0) % CHUNK
    ci2 = lax.broadcasted_iota(jnp.int32, (2 * CHUNK, CHUNK), 1)
    same_blk = [(ri2 >> (l + 1)) == (ci2 >> (l + 1)) for l in range(CHUNK_LEVELS)]
    upper = [((ridx >> l) & 1) == 1 for l in range(CHUNK_LEVELS)]
    tri = tri_ref[...]

    st = st_ref[...]
    outs = []
    for c in range(rows // CHUNK):
        sl = slice(c * CHUNK, (c + 1) * CHUNK)
        q = q_ref[sl, :] * (HEAD_DIM ** -0.5)
        k = k_ref[sl, :]
        v = v_ref[sl, :]
        g = log_a[sl, :]
        cum = _mm_exact_rhs(tri, g)
        last = cum[CHUNK - 1:CHUNK, :]
        q_dec = q * jnp.exp(cum)
        k_dec = k * jnp.exp(last - cum)
        e_last = jnp.exp(last)
        decays = _gla_level_decays(g, cum, ridx)

        a_st = jnp.zeros((2 * CHUNK, CHUNK), F32)
        for l in range(CHUNK_LEVELS):
            e = decays[l]
            ql = jnp.where(upper[l], q * e, 0.0)
            kl = jnp.where(upper[l], 0.0, k * e)
            lhs = jnp.concatenate([jnp.where(head0, ql, 0.0), jnp.where(head0, 0.0, ql)], axis=0)
            a_st = a_st + jnp.where(same_blk[l], _mm_nt(lhs, kl), 0.0)
        o_full = _mm(a_st, v)
        o_intra = jnp.where(head0, o_full[0:CHUNK], o_full[CHUNK:2 * CHUNK])
        diag = _mm(q * k, head_ones_f)
        outs.append(o_intra + diag * v + _mm_nt(q_dec, st))
        st = st * e_last + jnp.where(head_ones, _mm_tn(v, k_dec), 0.0)
    st_ref[...] = st

    o_all = jnp.concatenate(outs, axis=0)
    ms = _mm(o_all * o_all, head_ones_f) * (1.0 / HEAD_DIM)
    y = o_all * lax.rsqrt(ms + RMS_EPS) * nw_ref[...]
    o_ref[...] = y * _silu(z_ref[...])


def _gla(pg, wg, bg, nw, tri, batch, seq, rows):
    t = pg.shape[0]
    nt = seq // rows
    pairs = GLA_W // LANES
    col = lambda off: (lambda b, p, i: (b * nt + i, off + p))
    fixed = lambda b, p, i: (0, 0)
    blk = lambda off: pl.BlockSpec((rows, LANES), col(off))
    return pl.pallas_call(
        functools.partial(_gla_kernel, rows=rows),
        grid=(batch, pairs, nt),
        in_specs=[blk(0), blk(pairs), blk(2 * pairs), blk(3 * pairs),
                  pl.BlockSpec((rows, LANES), lambda b, p, i: (b * nt + i, 4 * pairs)),
                  pl.BlockSpec((LANES, LANES), lambda b, p, i: (0, p)),
                  pl.BlockSpec((1, LANES), lambda b, p, i: (0, p)),
                  pl.BlockSpec((1, LANES), fixed),
                  pl.BlockSpec(tri.shape, fixed)],
        out_specs=pl.BlockSpec((rows, LANES), lambda b, p, i: (b * nt + i, p)),
        out_shape=jax.ShapeDtypeStruct((t, GLA_W), F32),
        scratch_shapes=[pltpu.VMEM((LANES, LANES), F32)],
        compiler_params=_params(("arbitrary", "arbitrary", "arbitrary")),
        name="gla",
    )(pg, pg, pg, pg, pg, wg, bg, nw, tri)


def _gdn_kernel(q_ref, k_ref, v_ref, z_ref, e_ref, cq_ref, ck_ref, cv_ref, alog_ref, dt_ref,
                nw_ref, tri_ref, indb_ref, indg_ref, o_ref, bq_ref, bk_ref, bv_ref, st_ref,
                *, rows):
    w = GDN_W
    halo = 8

    @pl.when(pl.program_id(1) == 0)
    def _():
        st_ref[...] = jnp.zeros_like(st_ref)
        for buf in (bq_ref, bk_ref, bv_ref):
            buf[0:halo, :] = jnp.zeros((halo, w), F32)

    def conv_silu(x_ref, buf, cw_ref):
        buf[halo:halo + rows, :] = x_ref[...]
        acc = jnp.zeros((rows, w), F32)
        for i in range(CONV_WIDTH):
            acc = acc + cw_ref[i:i + 1, :] * buf[pl.ds(halo - (CONV_WIDTH - 1) + i, rows), :]
        buf[0:halo, :] = buf[rows:rows + halo, :]
        return _silu(acc)

    q = conv_silu(q_ref, bq_ref, cq_ref)
    k = conv_silu(k_ref, bk_ref, ck_ref)
    v = conv_silu(v_ref, bv_ref, cv_ref)

    head_ones = _block_diag_mask(w, w, HEAD_DIM, HEAD_DIM)
    head_ones_f = head_ones.astype(MXU_DTYPE)

    sumsq = lambda x: _mm(x * x, head_ones_f)
    q = q * lax.rsqrt(sumsq(q) + RMS_EPS) * (HEAD_DIM ** -0.5)
    k = k * lax.rsqrt(sumsq(k) + RMS_EPS)

    extra = e_ref[...]
    beta_e = _mm_exact_lhs3(_sigmoid(extra), indb_ref[...])
    g_small = -jnp.exp(alog_ref[...]) * _softplus(extra + dt_ref[...])

    lane = lax.broadcasted_iota(jnp.int32, (CHUNK, w), 1) % HEAD_DIM
    ridx = lax.broadcasted_iota(jnp.int32, (CHUNK, w), 0)
    incl = lane <= ridx
    strict = lane < ridx
    tri = tri_ref[...]
    indg = indg_ref[...]

    bd = lambda x: jnp.where(head_ones, _tile_rows(x, GDN_HEADS), 0.0)
    eye_all = (lane == ridx).astype(F32)
    nchunk = rows // CHUNK

    xs, ps, keep = [], [], []
    for c in range(nchunk):
        sl = slice(c * CHUNK, (c + 1) * CHUNK)
        qc, kc, vc, bc = q[sl], k[sl], v[sl], beta_e[sl]
        h3 = _split3(g_small[sl])
        gc_small = _mm(tri, h3[0]) + _mm(tri, h3[1]) + _mm(tri, h3[2])
        gc = _mm_exact_lhs3(gc_small, indg)
        eg = jnp.exp(gc)
        g_last = gc[CHUNK - 1:CHUNK, :]
        kb = kc * bc

        g_hi, g_mid, g_lo = [p.astype(F32) for p in _split3(gc)]
        one = jnp.ones_like(gc)
        zero = jnp.zeros_like(gc)
        a2 = jnp.where(lane == 0, g_hi, jnp.where(lane == 1, g_mid, jnp.where(
            lane == 2, g_lo, jnp.where(lane < 6, one, zero))))
        b2 = jnp.where(lane < 3, one, jnp.where(lane == 3, -g_hi, jnp.where(
            lane == 4, -g_mid, jnp.where(lane == 5, -g_lo, zero))))
        decay = jnp.exp(jnp.where(incl, _mm_nt(a2, bd(b2)), NEG_BIG))

        kk = _mm_nt(jnp.concatenate([kb, qc], axis=0), bd(kc))
        xs.append(-jnp.where(strict, kk[0:CHUNK] * decay, 0.0))
        ps.append(eye_all)
        keep.append(dict(attn=kk[CHUNK:2 * CHUNK] * decay, vb=vc * bc, kbg=kb * eg, qg=qc * eg,
                         k_dec=kc * jnp.exp(g_last - gc), e_last=jnp.exp(g_last)))

    for lvl in range(CHUNK_LEVELS):
        for c in range(nchunk):
            rhs = bd(xs[c])
            if lvl < CHUNK_LEVELS - 1:
                r = _mm(jnp.concatenate([xs[c], ps[c]], axis=0), rhs)
                xs[c] = r[0:CHUNK]
                ps[c] = ps[c] + r[CHUNK:2 * CHUNK]
            else:
                ps[c] = ps[c] + _mm(ps[c], rhs)

    steps = []
    for c in range(nchunk):
        kp = keep[c]
        uw = _mm(ps[c], jnp.concatenate([bd(kp["vb"]), bd(kp["kbg"])], axis=1))
        u, wmat = uw[:, 0:w], uw[:, w:2 * w]
        ao = _mm(kp["attn"], jnp.concatenate([bd(wmat), bd(u)], axis=1))
        nb = _mm_tn(jnp.concatenate([wmat, u], axis=1), kp["k_dec"])
        steps.append((kp["qg"] - ao[:, 0:w], ao[:, w:2 * w], kp["e_last"],
                      jnp.where(head_ones, nb[0:w], 0.0), jnp.where(head_ones, nb[w:2 * w], 0.0)))

    st = st_ref[...]
    outs = []
    for q_eff, o_const, e_last, n_mat, b_mat in steps:
        outs.append(_mm_nt(q_eff, st) + o_const)
        st = st * e_last - _mm(st, n_mat) + b_mat
    st_ref[...] = st

    o_all = jnp.concatenate(outs, axis=0)
    ms = _mm(o_all * o_all, head_ones_f) * (1.0 / HEAD_DIM)
    y = o_all * lax.rsqrt(ms + RMS_EPS) * nw_ref[...]
    o_ref[...] = y * _silu(z_ref[...])


def _gdn(pd, cq, ck, cv, alog, dt, nw, tri, indb, indg, batch, seq, rows):
    t = pd.shape[0]
    nt = seq // rows
    w = GDN_W
    fixed = lambda b, i: (0, 0)
    blk = lambda off: pl.BlockSpec((rows, w), lambda b, i: (b * nt + i, off))
    small = lambda a: pl.BlockSpec(a.shape, fixed)
    return pl.pallas_call(
        functools.partial(_gdn_kernel, rows=rows),
        grid=(batch, nt),
        in_specs=[blk(0), blk(1), blk(2), blk(3),
                  pl.BlockSpec((rows, LANES), lambda b, i: (b * nt + i, 4 * w // LANES)),
                  small(cq), small(ck), small(cv), small(alog), small(dt), small(nw),
                  small(tri), small(indb), small(indg)],
        out_specs=pl.BlockSpec((rows, w), lambda b, i: (b * nt + i, 0)),
        out_shape=jax.ShapeDtypeStruct((t, w), F32),
        scratch_shapes=[pltpu.VMEM((rows + 8, w), F32), pltpu.VMEM((rows + 8, w), F32),
                        pltpu.VMEM((rows + 8, w), F32), pltpu.VMEM((w, w), F32)],
        compiler_params=_params(("arbitrary", "arbitrary")),
        name="gdn",
    )(pd, pd, pd, pd, pd, cq, ck, cv, alog, dt, nw, tri, indb, indg)


def _moba_prep_kernel(q_ref, k_ref, v_ref, cos_ref, sin_ref,
                      qo_ref, ko_ref, vt_ref, bias_ref, km_ref, *, nblk):
    n = pl.program_id(1)
    blk = MOBA_BLOCK
    hw = MOBA_W

    @pl.when(n == 0)
    def _():
        km_ref[...] = jnp.zeros_like(km_ref)

    cosf = cos_ref[...]
    sinf = sin_ref[...]
    lane = lax.broadcasted_iota(jnp.int32, (blk, LANES), 1)
    first_half = (lane % HEAD_DIM) < (HEAD_DIM // 2)

    def rope(x_ref):
        parts = []
        for c in range(hw // LANES):
            x = x_ref[:, c * LANES:(c + 1) * LANES]
            swapped = jnp.where(first_half, pltpu.roll(x, LANES - HEAD_DIM // 2, 1),
                                pltpu.roll(x, HEAD_DIM // 2, 1))
            parts.append(x * cosf + swapped * sinf)
        return jnp.concatenate(parts, axis=1)

    q = rope(q_ref) * (HEAD_DIM ** -0.5 * LOG2_E)
    k = rope(k_ref)

    km_rows = jnp.where(_block_diag_mask(MOBA_HEADS * nblk, hw, nblk, HEAD_DIM),
                        _tile_rows(km_ref[...], MOBA_HEADS), 0.0)
    km_hi, km_lo = _split2(km_rows)
    q_hi, q_lo = _split2(q)
    gate = _mm_nt(km_hi, q_hi) + _mm_nt(km_hi, q_lo) + _mm_nt(km_lo, q_hi)
    gate = gate.reshape(MOBA_HEADS, nblk, blk)
    kb = lax.broadcasted_iota(jnp.int32, (MOBA_HEADS, nblk, blk), 1)
    past = kb < n
    g = jnp.where(past, gate, -jnp.inf)
    sel = jnp.zeros((MOBA_HEADS, nblk, blk), jnp.bool_)
    for _ in range(MOBA_TOPK):
        m = jnp.max(g, axis=1, keepdims=True)
        first = jnp.min(jnp.where(g == m, kb, nblk), axis=1, keepdims=True)
        pick = kb == first
        sel = sel | (pick & past)
        g = jnp.where(pick, -jnp.inf, g)
    bias_ref[...] = jnp.where(sel, 0.0, NEG_BIG).reshape(MOBA_HEADS * nblk, blk)

    qo_ref[...] = q.astype(qo_ref.dtype)
    ko_ref[...] = k.astype(ko_ref.dtype)
    v_t = v_ref[...].T
    ones_row = (lax.broadcasted_iota(jnp.int32, (MOBA_VT_ROWS - HEAD_DIM, blk), 0) == 0).astype(F32)
    vt_ref[...] = jnp.concatenate(
        [part for h in range(MOBA_HEADS)
         for part in (v_t[h * HEAD_DIM:(h + 1) * HEAD_DIM], ones_row)], axis=0).astype(vt_ref.dtype)
    km_ref[pl.ds(n, 1), :] = jnp.mean(k, axis=0, keepdims=True)


def _moba_prep(pm, cos, sin, batch, seq):
    t = pm.shape[0]
    nblk = seq // MOBA_BLOCK
    hw = MOBA_W
    blk = lambda off: pl.BlockSpec((MOBA_BLOCK, hw), lambda b, i: (b * nblk + i, off))
    tab = pl.BlockSpec((MOBA_BLOCK, LANES), lambda b, i: (i, 0))
    row_out = pl.BlockSpec((MOBA_BLOCK, hw), lambda b, i: (b * nblk + i, 0))
    per_block = lambda r: pl.BlockSpec((None, None, r, MOBA_BLOCK), lambda b, i: (b, i, 0, 0))
    return pl.pallas_call(
        functools.partial(_moba_prep_kernel, nblk=nblk),
        grid=(batch, nblk),
        in_specs=[blk(0), blk(1), blk(2), tab, tab],
        out_specs=[row_out, row_out, per_block(MOBA_HEADS * MOBA_VT_ROWS),
                   per_block(MOBA_HEADS * nblk)],
        out_shape=[jax.ShapeDtypeStruct((t, hw), MXU_DTYPE),
                   jax.ShapeDtypeStruct((t, hw), MXU_DTYPE),
                   jax.ShapeDtypeStruct((batch, nblk, MOBA_HEADS * MOBA_VT_ROWS, MOBA_BLOCK),
                                        MXU_DTYPE),
                   jax.ShapeDtypeStruct((batch, nblk, MOBA_HEADS * nblk, MOBA_BLOCK), F32)],
        scratch_shapes=[pltpu.VMEM((nblk, hw), F32)],
        compiler_params=_params(("arbitrary", "arbitrary")),
        name="moba_prep",
    )(pm, pm, pm, cos, sin)


def _moba_attn_kernel(q_ref, k_ref, vt_ref, bias_ref, o_ref, sa_ref, sb_ref, *, nblk):
    m_step = pl.program_id(2)
    blk = MOBA_BLOCK
    two = 2 * blk
    heads = range(2)
    key = lax.broadcasted_iota(jnp.int32, (two, two), 0)
    qry = lax.broadcasted_iota(jnp.int32, (two, two), 1)
    lane = lax.broadcasted_iota(jnp.int32, (two, LANES), 1)
    hv = lambda h: slice(h * MOBA_VT_ROWS, (h + 1) * MOBA_VT_ROWS)
    q_pair = q_ref[...]
    q_head = [jnp.where((lane // HEAD_DIM) == h, q_pair, jnp.zeros_like(q_pair)) for h in heads]
    zero_row = jnp.zeros((1, blk), F32)

    def produce(buf, pair):
        k_pair = k_ref[pl.ds(pl.multiple_of(pair * two, two), two), :]
        for h in heads:
            buf[h] = _mm_nt(k_pair, q_head[h])

    def bias_rows(h, item):
        if item is None:
            blk_a = 2 * m_step
            row_a = jnp.concatenate([zero_row, bias_ref[1, pl.ds(h * nblk + blk_a, 1), :]], axis=1)
            return blk_a, blk_a + 1, row_a, jnp.zeros((1, two), F32)
        blk_a = 2 * (item - 1)
        rows = [jnp.concatenate([bias_ref[0, pl.ds(h * nblk + j, 1), :],
                                 bias_ref[1, pl.ds(h * nblk + j, 1), :]], axis=1)
                for j in (blk_a, blk_a + 1)]
        return blk_a, blk_a + 1, rows[0], rows[1]

    def consume(buf, item, carry):
        new = []
        for h in heads:
            m, acc = carry[2 * h], carry[2 * h + 1]
            blk_a, blk_b, row_a, row_b = bias_rows(h, item)
            s = buf[h]
            if item is None:
                s = jnp.where(key <= qry, s, NEG_BIG)
            s_a, s_b = s[0:blk], s[blk:two]
            m_new = jnp.maximum(m, jnp.maximum(jnp.max(s_a, axis=0, keepdims=True) + row_a,
                                               jnp.max(s_b, axis=0, keepdims=True) + row_b))
            p = jnp.concatenate([jnp.exp2(s_a - (m_new - row_a)),
                                 jnp.exp2(s_b - (m_new - row_b))], axis=0)
            vt = jnp.concatenate([vt_ref[blk_a, hv(h), :], vt_ref[blk_b, hv(h), :]], axis=1)
            new += [m_new, jnp.exp2(m - m_new) * acc + _mm(vt, p)]
        return tuple(new)

    init = (jnp.full((1, two), NEG_BIG, F32), jnp.zeros((MOBA_VT_ROWS, two), F32)) * 2
    produce(sa_ref, m_step)
    produce(sb_ref, 0)
    carry = consume(sa_ref, None, init)

    def body(t, carry):
        produce(sa_ref, 2 * t + 1)
        carry = consume(sb_ref, 2 * t + 1, carry)
        produce(sb_ref, 2 * t + 2)
        return consume(sa_ref, 2 * t + 2, carry)

    carry = lax.fori_loop(0, m_step // 2, body, carry)
    carry = lax.cond(m_step % 2 == 1, lambda c: consume(sb_ref, m_step, c), lambda c: c, carry)
    o_t = jnp.concatenate(
        [carry[2 * h + 1][0:HEAD_DIM] / carry[2 * h + 1][HEAD_DIM:HEAD_DIM + 1] for h in heads],
        axis=0)
    o_ref[...] = o_t.T


def _moba_attn(qr, kr, vt, bias, batch, seq):
    t = qr.shape[0]
    nblk = seq // MOBA_BLOCK
    nstep = nblk // 2
    pairs = MOBA_HEADS // 2
    two = 2 * MOBA_BLOCK
    return pl.pallas_call(
        functools.partial(_moba_attn_kernel, nblk=nblk),
        grid=(batch, pairs, nstep),
        in_specs=[pl.BlockSpec((two, LANES), lambda b, p, i: (b * nstep + i, p)),
                  pl.BlockSpec((seq, LANES), lambda b, p, i: (b, p)),
                  pl.BlockSpec((None, nblk, 2 * MOBA_VT_ROWS, MOBA_BLOCK),
                               lambda b, p, i: (b, 0, p, 0)),
                  pl.BlockSpec((None, 2, 2 * nblk, MOBA_BLOCK), lambda b, p, i: (b, i, p, 0))],
        out_specs=pl.BlockSpec((two, LANES), lambda b, p, i: (b * nstep + i, p)),
        out_shape=jax.ShapeDtypeStruct((t, MOBA_W), F32),
        scratch_shapes=[pltpu.VMEM((2, two, two), F32), pltpu.VMEM((2, two, two), F32)],
        compiler_params=_params(("arbitrary", "arbitrary", "arbitrary")),
        name="moba_attn",
    )(qr, kr, vt, bias)


def _tail_kernel(og_ref, od_ref, om_ref, x_ref, w1_ref, w2_ref, w3_ref, nmix_ref, npre_ref,
                 wg_ref, wu_ref, wd_ref, npost_ref, o_ref, *, col_chunks):
    mix = (_mm(og_ref[...], w1_ref[...]) + _mm(od_ref[...], w2_ref[...])
           + _mm(om_ref[...], w3_ref[...]))
    ms = jnp.mean(mix * mix, axis=-1, keepdims=True)
    x1 = x_ref[...] + mix * lax.rsqrt(ms + RMS_EPS) * nmix_ref[...]

    ms = jnp.mean(x1 * x1, axis=-1, keepdims=True)
    h = (x1 * lax.rsqrt(ms + RMS_EPS) * npre_ref[...]).astype(MXU_DTYPE)
    y = None
    for lo, hi in col_chunks:
        g = jnp.dot(h, wg_ref[:, lo:hi], preferred_element_type=F32)
        u = jnp.dot(h, wu_ref[:, lo:hi], preferred_element_type=F32)
        part = _mm(_silu(g) * u, wd_ref[lo:hi, :])
        y = part if y is None else y + part
    ms = jnp.mean(y * y, axis=-1, keepdims=True)
    o_ref[...] = x1 + y * lax.rsqrt(ms + RMS_EPS) * npost_ref[...]


def _tail(og, od, om, xf, w1, w2, w3, nmix, npre, wg, wu, wd, npost, tm, chunk):
    t, d = xf.shape
    dff = wg.shape[1]
    col_chunks = tuple((lo, min(lo + chunk, dff)) for lo in range(0, dff, chunk))
    row = lambda i: (i, 0)
    fixed = lambda i: (0, 0)
    resident = lambda a: pl.BlockSpec(a.shape, fixed, pipeline_mode=pl.Buffered(1))
    tile = lambda a: pl.BlockSpec((tm, a.shape[1]), row)
    return pl.pallas_call(
        functools.partial(_tail_kernel, col_chunks=col_chunks),
        grid=(t // tm,),
        in_specs=[tile(og), tile(od), tile(om), tile(xf), resident(w1), resident(w2),
                  resident(w3), resident(nmix), resident(npre), resident(wg), resident(wu),
                  resident(wd), resident(npost)],
        out_specs=pl.BlockSpec((tm, d), row),
        out_shape=jax.ShapeDtypeStruct((t, d), F32),
        compiler_params=_params(("arbitrary",)),
        name="outproj_ffn",
    )(og, od, om, xf, w1, w2, w3, nmix, npre, wg, wu, wd, npost)


def _tile_sizes(tokens, seq):
    pick = lambda want, n: want if n % want == 0 else MOBA_BLOCK
    return dict(proj=pick(512, tokens), ffn=pick(512, tokens), rec=pick(512, seq))


def _rope_tables(seq):
    half = HEAD_DIM // 2
    inv = ROPE_THETA ** (-jnp.arange(half, dtype=F32) / half)
    ang = jnp.arange(seq).astype(F32)[:, None] * inv[None, :]
    cos, sin = jnp.cos(ang), jnp.sin(ang)
    reps = LANES // HEAD_DIM
    return (jnp.tile(jnp.concatenate([cos, cos], axis=1), (1, reps)),
            jnp.tile(jnp.concatenate([-sin, sin], axis=1), (1, reps)))


def _head_indicator(first_lane, heads):
    ind = np.zeros((LANES, heads * HEAD_DIM), np.float32)
    for h in range(heads):
        ind[first_lane + h, h * HEAD_DIM:(h + 1) * HEAD_DIM] = 1.0
    return jnp.asarray(ind, MXU_DTYPE)


def _lane_row(values, first_lane):
    return jnp.zeros((1, LANES), F32).at[0, first_lane:first_lane + values.shape[0]].set(
        values.astype(F32))


def kernel(x, norm_mix_pre, norm_mix_post, norm_ffn_pre, norm_ffn_post, w_in, w_o, gla_w_gate,
           gla_b_gate, gla_norm, gdn_conv, gdn_a_log, gdn_dt_bias, gdn_norm, ffn_w_gate, ffn_w_up,
           ffn_w_down):
    batch, seq, d_model = x.shape
    depth = w_in.shape[0]
    tokens = batch * seq
    assert seq % (2 * MOBA_BLOCK) == 0
    ts = _tile_sizes(tokens, seq)

    cos, sin = _rope_tables(seq)
    tri = jnp.asarray(np.tril(np.ones((CHUNK, CHUNK), np.float32)), MXU_DTYPE)
    indb = _head_indicator(0, GDN_HEADS)
    indg = _head_indicator(GDN_HEADS, GDN_HEADS)
    row2d = lambda a: a.reshape(1, -1).astype(F32)
    pad_cols = lambda a, n: jnp.pad(a, ((0, 0), (0, n - a.shape[1])))
    bf = lambda a: a.astype(MXU_DTYPE)

    o_ga = 4 * GLA_W
    o_d = o_ga + GLA_GATE_RANK
    o_db = o_d + 4 * GDN_W
    o_m = o_db + 2 * GDN_HEADS

    xf = x.reshape(tokens, d_model)
    for l in range(depth):
        wl = w_in[l]
        w_gla = bf(jnp.concatenate([wl[:, :o_ga], pad_cols(wl[:, o_ga:o_d], LANES)], axis=1))
        w_gdn = bf(jnp.concatenate([wl[:, o_d:o_db], pad_cols(wl[:, o_db:o_m], LANES)], axis=1))
        w_moba = bf(wl[:, o_m:])
        pg, pd, pm = _inproj(xf, row2d(norm_mix_pre[l]), w_gla, w_gdn, w_moba, ts["proj"])

        wgate = bf(jnp.pad(gla_w_gate[l], ((0, LANES - GLA_GATE_RANK), (0, 0))))
        o_gla = _gla(pg, wgate, row2d(gla_b_gate[l]),
                     row2d(jnp.tile(gla_norm[l], LANES // HEAD_DIM)), tri, batch, seq, ts["rec"])

        conv = gdn_conv[l].astype(F32)
        o_gdn = _gdn(pd, conv[:, :GDN_W], conv[:, GDN_W:2 * GDN_W], conv[:, 2 * GDN_W:],
                     _lane_row(gdn_a_log[l], GDN_HEADS), _lane_row(gdn_dt_bias[l], GDN_HEADS),
                     row2d(jnp.tile(gdn_norm[l], GDN_HEADS)), tri, indb, indg,
                     batch, seq, ts["rec"])

        o_moba = _moba_attn(*_moba_prep(pm, cos, sin, batch, seq), batch, seq)

        wo = bf(w_o[l])
        xf = _tail(o_gla, o_gdn, o_moba, xf, wo[:GLA_W], wo[GLA_W:GLA_W + GDN_W],
                   wo[GLA_W + GDN_W:], row2d(norm_mix_post[l]), row2d(norm_ffn_pre[l]),
                   bf(ffn_w_gate[l]), bf(ffn_w_up[l]), bf(ffn_w_down[l]),
                   row2d(norm_ffn_post[l]), ts["ffn"], FFN_CHUNK)
    return xf.reshape(batch, seq, d_model)
```

```python
import functools

import numpy as np
import jax
import jax.numpy as jnp
from jax import lax
from jax.experimental import pallas as pl
from jax.experimental.pallas import tpu as pltpu

F32 = jnp.float32
MXU_DTYPE = jnp.bfloat16

HEAD_DIM = 64
GLA_HEADS = 4
GDN_HEADS = 4
MOBA_HEADS = 8
GLA_W = GLA_HEADS * HEAD_DIM
GDN_W = GDN_HEADS * HEAD_DIM
MOBA_W = MOBA_HEADS * HEAD_DIM
GLA_GATE_RANK = 16
GLA_GATE_TAU = 16.0
CHUNK = 64
CHUNK_LEVELS = 6
CONV_WIDTH = 4
MOBA_BLOCK = 256
MOBA_TOPK = 3
MOBA_STEP_HEADS = 2
MOBA_VT_ROWS = 80
ROPE_THETA = 10000.0
RMS_EPS = 1e-6
LANES = 128
NEG_BIG = -1e30
LOG2_E = 1.4426950408889634
VMEM_LIMIT = 48 * 1024 * 1024
FFN_CHUNK = 512


def _mm(a, b):
    return jnp.dot(a.astype(MXU_DTYPE), b.astype(MXU_DTYPE), preferred_element_type=F32)


def _mm_nt(a, b):
    return lax.dot_general(a.astype(MXU_DTYPE), b.astype(MXU_DTYPE),
                           (((1,), (1,)), ((), ())), preferred_element_type=F32)


def _mm_tn(a, b):
    return lax.dot_general(a.astype(MXU_DTYPE), b.astype(MXU_DTYPE),
                           (((0,), (0,)), ((), ())), preferred_element_type=F32)


def _split2(x):
    hi = x.astype(MXU_DTYPE)
    lo = (x - hi.astype(F32)).astype(MXU_DTYPE)
    return hi, lo


def _split3(x):
    hi = x.astype(MXU_DTYPE)
    r = x - hi.astype(F32)
    mid = r.astype(MXU_DTYPE)
    lo = (r - mid.astype(F32)).astype(MXU_DTYPE)
    return hi, mid, lo


def _mm_exact_rhs(c, x):
    hi, lo = _split2(x)
    return _mm(c, hi) + _mm(c, lo)


def _mm_exact_lhs3(x, c):
    hi, mid, lo = _split3(x)
    return _mm(hi, c) + _mm(mid, c) + _mm(lo, c)


def _sigmoid(x):
    return 1.0 / (1.0 + jnp.exp(-x))


def _softplus(x):
    return jnp.maximum(x, 0.0) + jnp.log(1.0 + jnp.exp(-jnp.abs(x)))


def _silu(x):
    return x * _sigmoid(x)


def _tile_rows(x, n):
    return jnp.concatenate([x] * n, axis=0)


def _block_diag_mask(rows, cols, rblk, cblk):
    r = lax.broadcasted_iota(jnp.int32, (rows, cols), 0) // rblk
    c = lax.broadcasted_iota(jnp.int32, (rows, cols), 1) // cblk
    return r == c


def _params(sem):
    return pltpu.CompilerParams(dimension_semantics=sem, vmem_limit_bytes=VMEM_LIMIT)


def _inproj_kernel(x_ref, nw_ref, w1_ref, w2_ref, w3_ref, o1_ref, o2_ref, o3_ref):
    x = x_ref[...]
    ms = jnp.mean(x * x, axis=-1, keepdims=True)
    h = (x * lax.rsqrt(ms + RMS_EPS) * nw_ref[...]).astype(MXU_DTYPE)
    o1_ref[...] = jnp.dot(h, w1_ref[...], preferred_element_type=F32)
    o2_ref[...] = jnp.dot(h, w2_ref[...], preferred_element_type=F32)
    o3_ref[...] = jnp.dot(h, w3_ref[...], preferred_element_type=F32)


def _inproj(xf, nw, w1, w2, w3, tm):
    t, d = xf.shape
    n1, n2, n3 = w1.shape[1], w2.shape[1], w3.shape[1]
    row = lambda i: (i, 0)
    fixed = lambda i: (0, 0)
    return pl.pallas_call(
        _inproj_kernel,
        grid=(t // tm,),
        in_specs=[pl.BlockSpec((tm, d), row), pl.BlockSpec((1, d), fixed),
                  pl.BlockSpec((d, n1), fixed), pl.BlockSpec((d, n2), fixed),
                  pl.BlockSpec((d, n3), fixed)],
        out_specs=[pl.BlockSpec((tm, n1), row), pl.BlockSpec((tm, n2), row),
                   pl.BlockSpec((tm, n3), row)],
        out_shape=[jax.ShapeDtypeStruct((t, n1), F32), jax.ShapeDtypeStruct((t, n2), F32),
                   jax.ShapeDtypeStruct((t, n3), F32)],
        compiler_params=_params(("arbitrary",)),
        name="inproj",
    )(xf, nw, w1, w2, w3)


def _gla_level_decays(g, cum, ridx):
    prev = pltpu.roll(g, 1, 0)
    nxt = pltpu.roll(g, CHUNK - 1, 0)
    r4 = ridx % 4
    logs = [jnp.where(ridx % 2 == 1, g, 0.0),
            jnp.where(r4 == 0, nxt, jnp.where(r4 == 1, 0.0, jnp.where(r4 == 2, g, g + prev)))]
    for lvl in range(2, CHUNK_LEVELS):
        half = 1 << lvl
        ref = jnp.concatenate(
            [jnp.broadcast_to(cum[b + half - 1:b + half, :], (2 * half, LANES))
             for b in range(0, CHUNK, 2 * half)], axis=0)
        logs.append(-jnp.abs(cum - ref))
    return [jnp.exp(d) for d in logs]


def _gla_kernel(q_ref, k_ref, v_ref, z_ref, a_ref, wg_ref, bg_ref, nw_ref, tri_ref,
                o_ref, st_ref, *, rows):
    @pl.when(pl.program_id(2) == 0)
    def _():
        st_ref[...] = jnp.zeros_like(st_ref)

    pre = _mm(a_ref[...], wg_ref[...]) + bg_ref[...]
    log_a = -_softplus(-pre) * (1.0 / GLA_GATE_TAU)

    lane = lax.broadcasted_iota(jnp.int32, (CHUNK, LANES), 1)
    head0 = lane < HEAD_DIM
    ridx = lax.broadcasted_iota(jnp.int32, (CHUNK, LANES), 0)
    head_ones = _block_diag_mask(LANES, LANES, HEAD_DIM, HEAD_DIM)
    head_ones_f = head_ones.astype(MXU_DTYPE)
    ri2 = lax.broadcasted_iota(jnp.int32, (2 * CHUNK, CHUNK), 0) % CHUNK
    ci2 = lax.broadcasted_iota(jnp.int32, (2 * CHUNK, CHUNK), 1)
    same_blk = [(ri2 >> (l + 1)) == (ci2 >> (l + 1)) for l in range(CHUNK_LEVELS)]
    upper = [((ridx >> l) & 1) == 1 for l in range(CHUNK_LEVELS)]
    tri = tri_ref[...]

    st = st_ref[...]
    outs = []
    for c in range(rows // CHUNK):
        sl = slice(c * CHUNK, (c + 1) * CHUNK)
        q = q_ref[sl, :] * (HEAD_DIM ** -0.5)
        k = k_ref[sl, :]
        v = v_ref[sl, :]
        g = log_a[sl, :]
        cum = _mm_exact_rhs(tri, g)
        last = cum[CHUNK - 1:CHUNK, :]
        q_dec = q * jnp.exp(cum)
        k_dec = k * jnp.exp(last - cum)
        e_last = jnp.exp(last)
        decays = _gla_level_decays(g, cum, ridx)

        a_st = jnp.zeros((2 * CHUNK, CHUNK), F32)
        for l in range(CHUNK_LEVELS):
            e = decays[l]
            ql = jnp.where(upper[l], q * e, 0.0)
            kl = jnp.where(upper[l], 0.0, k * e)
            lhs = jnp.concatenate([jnp.where(head0, ql, 0.0), jnp.where(head0, 0.0, ql)], axis=0)
            a_st = a_st + jnp.where(same_blk[l], _mm_nt(lhs, kl), 0.0)
        o_full = _mm(a_st, v)
        o_intra = jnp.where(head0, o_full[0:CHUNK], o_full[CHUNK:2 * CHUNK])
        diag = _mm(q * k, head_ones_f)
        outs.append(o_intra + diag * v + _mm_nt(q_dec, st))
        st = st * e_last + jnp.where(head_ones, _mm_tn(v, k_dec), 0.0)
    st_ref[...] = st

    o_all = jnp.concatenate(outs, axis=0)
    ms = _mm(o_all * o_all, head_ones_f) * (1.0 / HEAD_DIM)
    y = o_all * lax.rsqrt(ms + RMS_EPS) * nw_ref[...]
    o_ref[...] = y * _silu(z_ref[...])


def _gla(pg, wg, bg, nw, tri, batch, seq, rows):
    t = pg.shape[0]
    nt = seq // rows
    pairs = GLA_W // LANES
    col = lambda off: (lambda b, p, i: (b * nt + i, off + p))
    fixed = lambda b, p, i: (0, 0)
    blk = lambda off: pl.BlockSpec((rows, LANES), col(off))
    return pl.pallas_call(
        functools.partial(_gla_kernel, rows=rows),
        grid=(batch, pairs, nt),
        in_specs=[blk(0), blk(pairs), blk(2 * pairs), blk(3 * pairs),
                  pl.BlockSpec((rows, LANES), lambda b, p, i: (b * nt + i, 4 * pairs)),
                  pl.BlockSpec((LANES, LANES), lambda b, p, i: (0, p)),
                  pl.BlockSpec((1, LANES), lambda b, p, i: (0, p)),
                  pl.BlockSpec((1, LANES), fixed),
                  pl.BlockSpec(tri.shape, fixed)],
        out_specs=pl.BlockSpec((rows, LANES), lambda b, p, i: (b * nt + i, p)),
        out_shape=jax.ShapeDtypeStruct((t, GLA_W), F32),
        scratch_shapes=[pltpu.VMEM((LANES, LANES), F32)],
        compiler_params=_params(("arbitrary", "arbitrary", "arbitrary")),
        name="gla",
    )(pg, pg, pg, pg, pg, wg, bg, nw, tri)


def _gdn_kernel(q_ref, k_ref, v_ref, z_ref, e_ref, cq_ref, ck_ref, cv_ref, alog_ref, dt_ref,
                nw_ref, tri_ref, indb_ref, indg_ref, o_ref, bq_ref, bk_ref, bv_ref, st_ref,
                *, rows):
    w = GDN_W
    halo = 8

    @pl.when(pl.program_id(1) == 0)
    def _():
        st_ref[...] = jnp.zeros_like(st_ref)
        for buf in (bq_ref, bk_ref, bv_ref):
            buf[0:halo, :] = jnp.zeros((halo, w), F32)

    def conv_silu(x_ref, buf, cw_ref):
        buf[halo:halo + rows, :] = x_ref[...]
        acc = jnp.zeros((rows, w), F32)
        for i in range(CONV_WIDTH):
            acc = acc + cw_ref[i:i + 1, :] * buf[pl.ds(halo - (CONV_WIDTH - 1) + i, rows), :]
        buf[0:halo, :] = buf[rows:rows + halo, :]
        return _silu(acc)

    q = conv_silu(q_ref, bq_ref, cq_ref)
    k = conv_silu(k_ref, bk_ref, ck_ref)
    v = conv_silu(v_ref, bv_ref, cv_ref)

    head_ones = _block_diag_mask(w, w, HEAD_DIM, HEAD_DIM)
    head_ones_f = head_ones.astype(MXU_DTYPE)

    sumsq = lambda x: _mm(x * x, head_ones_f)
    q = q * lax.rsqrt(sumsq(q) + RMS_EPS) * (HEAD_DIM ** -0.5)
    k = k * lax.rsqrt(sumsq(k) + RMS_EPS)

    extra = e_ref[...]
    beta_e = _mm_exact_lhs3(_sigmoid(extra), indb_ref[...])
    g_small = -jnp.exp(alog_ref[...]) * _softplus(extra + dt_ref[...])

    lane = lax.broadcasted_iota(jnp.int32, (CHUNK, w), 1) % HEAD_DIM
    ridx = lax.broadcasted_iota(jnp.int32, (CHUNK, w), 0)
    incl = lane <= ridx
    strict = lane < ridx
    tri = tri_ref[...]
    indg = indg_ref[...]

    bd = lambda x: jnp.where(head_ones, _tile_rows(x, GDN_HEADS), 0.0)
    eye_all = (lane == ridx).astype(F32)
    nchunk = rows // CHUNK

    xs, ps, keep = [], [], []
    for c in range(nchunk):
        sl = slice(c * CHUNK, (c + 1) * CHUNK)
        qc, kc, vc, bc = q[sl], k[sl], v[sl], beta_e[sl]
        h3 = _split3(g_small[sl])
        gc_small = _mm(tri, h3[0]) + _mm(tri, h3[1]) + _mm(tri, h3[2])
        gc = _mm_exact_lhs3(gc_small, indg)
        eg = jnp.exp(gc)
        g_last = gc[CHUNK - 1:CHUNK, :]
        kb = kc * bc

        g_hi, g_mid, g_lo = [p.astype(F32) for p in _split3(gc)]
        one = jnp.ones_like(gc)
        zero = jnp.zeros_like(gc)
        a2 = jnp.where(lane == 0, g_hi, jnp.where(lane == 1, g_mid, jnp.where(
            lane == 2, g_lo, jnp.where(lane < 6, one, zero))))
        b2 = jnp.where(lane < 3, one, jnp.where(lane == 3, -g_hi, jnp.where(
            lane == 4, -g_mid, jnp.where(lane == 5, -g_lo, zero))))
        decay = jnp.exp(jnp.where(incl, _mm_nt(a2, bd(b2)), NEG_BIG))

        kk = _mm_nt(jnp.concatenate([kb, qc], axis=0), bd(kc))
        xs.append(-jnp.where(strict, kk[0:CHUNK] * decay, 0.0))
        ps.append(eye_all)
        keep.append(dict(attn=kk[CHUNK:2 * CHUNK] * decay, vb=vc * bc, kbg=kb * eg, qg=qc * eg,
                         k_dec=kc * jnp.exp(g_last - gc), e_last=jnp.exp(g_last)))

    for lvl in range(CHUNK_LEVELS):
        for c in range(nchunk):
            rhs = bd(xs[c])
            if lvl < CHUNK_LEVELS - 1:
                r = _mm(jnp.concatenate([xs[c], ps[c]], axis=0), rhs)
                xs[c] = r[0:CHUNK]
                ps[c] = ps[c] + r[CHUNK:2 * CHUNK]
            else:
                ps[c] = ps[c] + _mm(ps[c], rhs)

    steps = []
    for c in range(nchunk):
        kp = keep[c]
        uw = _mm(ps[c], jnp.concatenate([bd(kp["vb"]), bd(kp["kbg"])], axis=1))
        u, wmat = uw[:, 0:w], uw[:, w:2 * w]
        ao = _mm(kp["attn"], jnp.concatenate([bd(wmat), bd(u)], axis=1))
        nb = _mm_tn(jnp.concatenate([wmat, u], axis=1), kp["k_dec"])
        steps.append((kp["qg"] - ao[:, 0:w], ao[:, w:2 * w], kp["e_last"],
                      jnp.where(head_ones, nb[0:w], 0.0), jnp.where(head_ones, nb[w:2 * w], 0.0)))

    st = st_ref[...]
    outs = []
    for q_eff, o_const, e_last, n_mat, b_mat in steps:
        outs.append(_mm_nt(q_eff, st) + o_const)
        st = st * e_last - _mm(st, n_mat) + b_mat
    st_ref[...] = st

    o_all = jnp.concatenate(outs, axis=0)
    ms = _mm(o_all * o_all, head_ones_f) * (1.0 / HEAD_DIM)
    y = o_all * lax.rsqrt(ms + RMS_EPS) * nw_ref[...]
    o_ref[...] = y * _silu(z_ref[...])


def _gdn(pd, cq, ck, cv, alog, dt, nw, tri, indb, indg, batch, seq, rows):
    t = pd.shape[0]
    nt = seq // rows
    w = GDN_W
    fixed = lambda b, i: (0, 0)
    blk = lambda off: pl.BlockSpec((rows, w), lambda b, i: (b * nt + i, off))
    small = lambda a: pl.BlockSpec(a.shape, fixed)
    return pl.pallas_call(
        functools.partial(_gdn_kernel, rows=rows),
        grid=(batch, nt),
        in_specs=[blk(0), blk(1), blk(2), blk(3),
                  pl.BlockSpec((rows, LANES), lambda b, i: (b * nt + i, 4 * w // LANES)),
                  small(cq), small(ck), small(cv), small(alog), small(dt), small(nw),
                  small(tri), small(indb), small(indg)],
        out_specs=pl.BlockSpec((rows, w), lambda b, i: (b * nt + i, 0)),
        out_shape=jax.ShapeDtypeStruct((t, w), F32),
        scratch_shapes=[pltpu.VMEM((rows + 8, w), F32), pltpu.VMEM((rows + 8, w), F32),
                        pltpu.VMEM((rows + 8, w), F32), pltpu.VMEM((w, w), F32)],
        compiler_params=_params(("arbitrary", "arbitrary")),
        name="gdn",
    )(pd, pd, pd, pd, pd, cq, ck, cv, alog, dt, nw, tri, indb, indg)


def _moba_prep_kernel(q_ref, k_ref, v_ref, cos_ref, sin_ref,
                      qo_ref, ko_ref, vt_ref, bias_ref, km_ref, *, nblk):
    n = pl.program_id(1)
    blk = MOBA_BLOCK
    hw = MOBA_W

    @pl.when(n == 0)
    def _():
        km_ref[...] = jnp.zeros_like(km_ref)

    cosf = cos_ref[...]
    sinf = sin_ref[...]
    lane = lax.broadcasted_iota(jnp.int32, (blk, LANES), 1)
    first_half = (lane % HEAD_DIM) < (HEAD_DIM // 2)

    def rope(x_ref):
        parts = []
        for c in range(hw // LANES):
            x = x_ref[:, c * LANES:(c + 1) * LANES]
            swapped = jnp.where(first_half, pltpu.roll(x, LANES - HEAD_DIM // 2, 1),
                                pltpu.roll(x, HEAD_DIM // 2, 1))
            parts.append(x * cosf + swapped * sinf)
        return jnp.concatenate(parts, axis=1)

    q = rope(q_ref) * (HEAD_DIM ** -0.5 * LOG2_E)
    k = rope(k_ref)

    km_rows = jnp.where(_block_diag_mask(MOBA_HEADS * nblk, hw, nblk, HEAD_DIM),
                        _tile_rows(km_ref[...], MOBA_HEADS), 0.0)
    km_hi, km_lo = _split2(km_rows)
    q_hi, q_lo = _split2(q)
    gate = _mm_nt(km_hi, q_hi) + _mm_nt(km_hi, q_lo) + _mm_nt(km_lo, q_hi)
    gate = gate.reshape(MOBA_HEADS, nblk, blk)
    kb = lax.broadcasted_iota(jnp.int32, (MOBA_HEADS, nblk, blk), 1)
    past = kb < n
    g = jnp.where(past, gate, -jnp.inf)
    sel = jnp.zeros((MOBA_HEADS, nblk, blk), jnp.bool_)
    for _ in range(MOBA_TOPK):
        m = jnp.max(g, axis=1, keepdims=True)
        first = jnp.min(jnp.where(g == m, kb, nblk), axis=1, keepdims=True)
        pick = kb == first
        sel = sel | (pick & past)
        g = jnp.where(pick, -jnp.inf, g)
    bias_ref[...] = jnp.where(sel, 0.0, NEG_BIG).reshape(MOBA_HEADS * nblk, blk)

    qo_ref[...] = q.astype(qo_ref.dtype)
    ko_ref[...] = k.astype(ko_ref.dtype)
    v_t = v_ref[...].T
    ones_row = (lax.broadcasted_iota(jnp.int32, (MOBA_VT_ROWS - HEAD_DIM, blk), 0) == 0).astype(F32)
    vt_ref[...] = jnp.concatenate(
        [part for h in range(MOBA_HEADS)
         for part in (v_t[h * HEAD_DIM:(h + 1) * HEAD_DIM], ones_row)], axis=0).astype(vt_ref.dtype)
    km_ref[pl.ds(n, 1), :] = jnp.mean(k, axis=0, keepdims=True)


def _moba_prep(pm, cos, sin, batch, seq):
    t = pm.shape[0]
    nblk = seq // MOBA_BLOCK
    hw = MOBA_W
    blk = lambda off: pl.BlockSpec((MOBA_BLOCK, hw), lambda b, i: (b * nblk + i, off))
    tab = pl.BlockSpec((MOBA_BLOCK, LANES), lambda b, i: (i, 0))
    row_out = pl.BlockSpec((MOBA_BLOCK, hw), lambda b, i: (b * nblk + i, 0))
    per_block = lambda r: pl.BlockSpec((None, None, r, MOBA_BLOCK), lambda b, i: (b, i, 0, 0))
    return pl.pallas_call(
        functools.partial(_moba_prep_kernel, nblk=nblk),
        grid=(batch, nblk),
        in_specs=[blk(0), blk(1), blk(2), tab, tab],
        out_specs=[row_out, row_out, per_block(MOBA_HEADS * MOBA_VT_ROWS),
                   per_block(MOBA_HEADS * nblk)],
        out_shape=[jax.ShapeDtypeStruct((t, hw), MXU_DTYPE),
                   jax.ShapeDtypeStruct((t, hw), MXU_DTYPE),
                   jax.ShapeDtypeStruct((batch, nblk, MOBA_HEADS * MOBA_VT_ROWS, MOBA_BLOCK),
                                        MXU_DTYPE),
                   jax.ShapeDtypeStruct((batch, nblk, MOBA_HEADS * nblk, MOBA_BLOCK), F32)],
        scratch_shapes=[pltpu.VMEM((nblk, hw), F32)],
        compiler_params=_params(("arbitrary", "arbitrary")),
        name="moba_prep",
    )(pm, pm, pm, cos, sin)


def _moba_attn_kernel(q_ref, k_ref, vt_ref, bias_ref, o_ref, sa_ref, sb_ref, *, nblk):
    m_step = pl.program_id(2)
    blk = MOBA_BLOCK
    two = 2 * blk
    heads = range(MOBA_STEP_HEADS)
    key = lax.broadcasted_iota(jnp.int32, (two, two), 0)
    qry = lax.broadcasted_iota(jnp.int32, (two, two), 1)
    lane = lax.broadcasted_iota(jnp.int32, (two, MOBA_STEP_HEADS * HEAD_DIM), 1)
    hv = lambda h: slice(h * MOBA_VT_ROWS, (h + 1) * MOBA_VT_ROWS)
    q_pair = q_ref[...]
    q_head = [jnp.where((lane // HEAD_DIM) == h, q_pair, jnp.zeros_like(q_pair)) for h in heads]
    zero_row = jnp.zeros((1, blk), F32)

    def produce(buf, pair):
        k_pair = k_ref[pl.ds(pl.multiple_of(pair * two, two), two), :]
        for h in heads:
            buf[h] = _mm_nt(k_pair, q_head[h])

    def bias_rows(h, item):
        if item is None:
            blk_a = 2 * m_step
            row_a = jnp.concatenate([zero_row, bias_ref[1, pl.ds(h * nblk + blk_a, 1), :]], axis=1)
            return blk_a, blk_a + 1, row_a, jnp.zeros((1, two), F32)
        blk_a = 2 * (item - 1)
        rows = [jnp.concatenate([bias_ref[0, pl.ds(h * nblk + j, 1), :],
                                 bias_ref[1, pl.ds(h * nblk + j, 1), :]], axis=1)
                for j in (blk_a, blk_a + 1)]
        return blk_a, blk_a + 1, rows[0], rows[1]

    def consume(buf, item, carry):
        new = []
        for h in heads:
            m, acc = carry[2 * h], carry[2 * h + 1]
            blk_a, blk_b, row_a, row_b = bias_rows(h, item)
            s = buf[h]
            if item is None:
                s = jnp.where(key <= qry, s, NEG_BIG)
            s_a, s_b = s[0:blk], s[blk:two]
            m_new = jnp.maximum(m, jnp.maximum(jnp.max(s_a, axis=0, keepdims=True) + row_a,
                                               jnp.max(s_b, axis=0, keepdims=True) + row_b))
            p = jnp.concatenate([jnp.exp2(s_a - (m_new - row_a)),
                                 jnp.exp2(s_b - (m_new - row_b))], axis=0)
            vt = jnp.concatenate([vt_ref[blk_a, hv(h), :], vt_ref[blk_b, hv(h), :]], axis=1)
            new += [m_new, jnp.exp2(m - m_new) * acc + _mm(vt, p)]
        return tuple(new)

    init = (jnp.full((1, two), NEG_BIG, F32), jnp.zeros((MOBA_VT_ROWS, two), F32)) * len(heads)
    produce(sa_ref, m_step)
    produce(sb_ref, 0)
    carry = consume(sa_ref, None, init)

    def body(t, carry):
        produce(sa_ref, 2 * t + 1)
        carry = consume(sb_ref, 2 * t + 1, carry)
        produce(sb_ref, 2 * t + 2)
        return consume(sa_ref, 2 * t + 2, carry)

    carry = lax.fori_loop(0, m_step // 2, body, carry)
    carry = lax.cond(m_step % 2 == 1, lambda c: consume(sb_ref, m_step, c), lambda c: c, carry)
    o_t = jnp.concatenate(
        [carry[2 * h + 1][0:HEAD_DIM] / carry[2 * h + 1][HEAD_DIM:HEAD_DIM + 1] for h in heads],
        axis=0)
    o_ref[...] = o_t.T


def _moba_attn(qr, kr, vt, bias, batch, seq):
    t = qr.shape[0]
    nblk = seq // MOBA_BLOCK
    nstep = nblk // 2
    nh = MOBA_STEP_HEADS
    groups = MOBA_HEADS // nh
    lanes = nh * HEAD_DIM
    two = 2 * MOBA_BLOCK
    return pl.pallas_call(
        functools.partial(_moba_attn_kernel, nblk=nblk),
        grid=(batch, groups, nstep),
        in_specs=[pl.BlockSpec((two, lanes), lambda b, p, i: (b * nstep + i, p)),
                  pl.BlockSpec((seq, lanes), lambda b, p, i: (b, p)),
                  pl.BlockSpec((None, nblk, nh * MOBA_VT_ROWS, MOBA_BLOCK),
                               lambda b, p, i: (b, 0, p, 0)),
                  pl.BlockSpec((None, 2, nh * nblk, MOBA_BLOCK), lambda b, p, i: (b, i, p, 0))],
        out_specs=pl.BlockSpec((two, lanes), lambda b, p, i: (b * nstep + i, p)),
        out_shape=jax.ShapeDtypeStruct((t, MOBA_W), F32),
        scratch_shapes=[pltpu.VMEM((nh, two, two), F32), pltpu.VMEM((nh, two, two), F32)],
        compiler_params=_params(("arbitrary", "arbitrary", "arbitrary")),
        name="moba_attn",
    )(qr, kr, vt, bias)


def _tail_kernel(og_ref, od_ref, om_ref, x_ref, w1_ref, w2_ref, w3_ref, nmix_ref, npre_ref,
                 wg_ref, wu_ref, wd_ref, npost_ref, o_ref, *, col_chunks):
    mix = (_mm(og_ref[...], w1_ref[...]) + _mm(od_ref[...], w2_ref[...])
           + _mm(om_ref[...], w3_ref[...]))
    ms = jnp.mean(mix * mix, axis=-1, keepdims=True)
    x1 = x_ref[...] + mix * lax.rsqrt(ms + RMS_EPS) * nmix_ref[...]

    ms = jnp.mean(x1 * x1, axis=-1, keepdims=True)
    h = (x1 * lax.rsqrt(ms + RMS_EPS) * npre_ref[...]).astype(MXU_DTYPE)
    y = None
    for lo, hi in col_chunks:
        g = jnp.dot(h, wg_ref[:, lo:hi], preferred_element_type=F32)
        u = jnp.dot(h, wu_ref[:, lo:hi], preferred_element_type=F32)
        part = _mm(_silu(g) * u, wd_ref[lo:hi, :])
        y = part if y is None else y + part
    ms = jnp.mean(y * y, axis=-1, keepdims=True)
    o_ref[...] = x1 + y * lax.rsqrt(ms + RMS_EPS) * npost_ref[...]


def _tail(og, od, om, xf, w1, w2, w3, nmix, npre, wg, wu, wd, npost, tm, chunk):
    t, d = xf.shape
    dff = wg.shape[1]
    col_chunks = tuple((lo, min(lo + chunk, dff)) for lo in range(0, dff, chunk))
    row = lambda i: (i, 0)
    fixed = lambda i: (0, 0)
    resident = lambda a: pl.BlockSpec(a.shape, fixed, pipeline_mode=pl.Buffered(1))
    tile = lambda a: pl.BlockSpec((tm, a.shape[1]), row)
    return pl.pallas_call(
        functools.partial(_tail_kernel, col_chunks=col_chunks),
        grid=(t // tm,),
        in_specs=[tile(og), tile(od), tile(om), tile(xf), resident(w1), resident(w2),
                  resident(w3), resident(nmix), resident(npre), resident(wg), resident(wu),
                  resident(wd), resident(npost)],
        out_specs=pl.BlockSpec((tm, d), row),
        out_shape=jax.ShapeDtypeStruct((t, d), F32),
        compiler_params=_params(("arbitrary",)),
        name="outproj_ffn",
    )(og, od, om, xf, w1, w2, w3, nmix, npre, wg, wu, wd, npost)


def _tile_sizes(tokens, seq):
    pick = lambda want, n: want if n % want == 0 else MOBA_BLOCK
    return dict(proj=pick(512, tokens), ffn=pick(512, tokens), rec=pick(1024, seq))


def _rope_tables(seq):
    half = HEAD_DIM // 2
    inv = ROPE_THETA ** (-jnp.arange(half, dtype=F32) / half)
    ang = jnp.arange(seq).astype(F32)[:, None] * inv[None, :]
    cos, sin = jnp.cos(ang), jnp.sin(ang)
    reps = LANES // HEAD_DIM
    return (jnp.tile(jnp.concatenate([cos, cos], axis=1), (1, reps)),
            jnp.tile(jnp.concatenate([-sin, sin], axis=1), (1, reps)))


def _head_indicator(first_lane, heads):
    ind = np.zeros((LANES, heads * HEAD_DIM), np.float32)
    for h in range(heads):
        ind[first_lane + h, h * HEAD_DIM:(h + 1) * HEAD_DIM] = 1.0
    return jnp.asarray(ind, MXU_DTYPE)


def _lane_row(values, first_lane):
    return jnp.zeros((1, LANES), F32).at[0, first_lane:first_lane + values.shape[0]].set(
        values.astype(F32))


def kernel(x, norm_mix_pre, norm_mix_post, norm_ffn_pre, norm_ffn_post, w_in, w_o, gla_w_gate,
           gla_b_gate, gla_norm, gdn_conv, gdn_a_log, gdn_dt_bias, gdn_norm, ffn_w_gate, ffn_w_up,
           ffn_w_down):
    batch, seq, d_model = x.shape
    depth = w_in.shape[0]
    tokens = batch * seq
    assert seq % (2 * MOBA_BLOCK) == 0
    ts = _tile_sizes(tokens, seq)

    cos, sin = _rope_tables(seq)
    tri = jnp.asarray(np.tril(np.ones((CHUNK, CHUNK), np.float32)), MXU_DTYPE)
    indb = _head_indicator(0, GDN_HEADS)
    indg = _head_indicator(GDN_HEADS, GDN_HEADS)
    row2d = lambda a: a.reshape(1, -1).astype(F32)
    pad_cols = lambda a, n: jnp.pad(a, ((0, 0), (0, n - a.shape[1])))
    bf = lambda a: a.astype(MXU_DTYPE)

    o_ga = 4 * GLA_W
    o_d = o_ga + GLA_GATE_RANK
    o_db = o_d + 4 * GDN_W
    o_m = o_db + 2 * GDN_HEADS

    xf = x.reshape(tokens, d_model)
    for l in range(depth):
        wl = w_in[l]
        w_gla = bf(jnp.concatenate([wl[:, :o_ga], pad_cols(wl[:, o_ga:o_d], LANES)], axis=1))
        w_gdn = bf(jnp.concatenate([wl[:, o_d:o_db], pad_cols(wl[:, o_db:o_m], LANES)], axis=1))
        w_moba = bf(wl[:, o_m:])
        pg, pd, pm = _inproj(xf, row2d(norm_mix_pre[l]), w_gla, w_gdn, w_moba, ts["proj"])

        wgate = bf(jnp.pad(gla_w_gate[l], ((0, LANES - GLA_GATE_RANK), (0, 0))))
        o_gla = _gla(pg, wgate, row2d(gla_b_gate[l]),
                     row2d(jnp.tile(gla_norm[l], LANES // HEAD_DIM)), tri, batch, seq, ts["rec"])

        conv = gdn_conv[l].astype(F32)
        o_gdn = _gdn(pd, conv[:, :GDN_W], conv[:, GDN_W:2 * GDN_W], conv[:, 2 * GDN_W:],
                     _lane_row(gdn_a_log[l], GDN_HEADS), _lane_row(gdn_dt_bias[l], GDN_HEADS),
                     row2d(jnp.tile(gdn_norm[l], GDN_HEADS)), tri, indb, indg,
                     batch, seq, ts["rec"])

        o_moba = _moba_attn(*_moba_prep(pm, cos, sin, batch, seq), batch, seq)

        wo = bf(w_o[l])
        xf = _tail(o_gla, o_gdn, o_moba, xf, wo[:GLA_W], wo[GLA_W:GLA_W + GDN_W],
                   wo[GLA_W + GDN_W:], row2d(norm_mix_post[l]), row2d(norm_ffn_pre[l]),
                   bf(ffn_w_gate[l]), bf(ffn_w_up[l]), bf(ffn_w_down[l]),
                   row2d(norm_ffn_post[l]), ts["ffn"], FFN_CHUNK)
    return xf.reshape(batch, seq, d_model)
```

```python
import functools

import numpy as np
import jax
import jax.numpy as jnp
from jax import lax
from jax.experimental import pallas as pl
from jax.experimental.pallas import tpu as pltpu

F32 = jnp.float32
MXU_DTYPE = jnp.bfloat16

HEAD_DIM = 64
GLA_HEADS = 4
GDN_HEADS = 4
MOBA_HEADS = 8
GLA_W = GLA_HEADS * HEAD_DIM
GDN_W = GDN_HEADS * HEAD_DIM
MOBA_W = MOBA_HEADS * HEAD_DIM
GLA_GATE_RANK = 16
GLA_GATE_TAU = 16.0
CHUNK = 64
CHUNK_LEVELS = 6
CONV_WIDTH = 4
MOBA_BLOCK = 256
MOBA_TOPK = 3
MOBA_STEP_HEADS = 2
MOBA_VT_ROWS = 80
ROPE_THETA = 10000.0
RMS_EPS = 1e-6
LANES = 128
NEG_BIG = -1e30
LOG2_E = 1.4426950408889634
VMEM_LIMIT = 48 * 1024 * 1024
FFN_CHUNK = 512


def _mm(a, b):
    return jnp.dot(a.astype(MXU_DTYPE), b.astype(MXU_DTYPE), preferred_element_type=F32)


def _mm_nt(a, b):
    return lax.dot_general(a.astype(MXU_DTYPE), b.astype(MXU_DTYPE),
                           (((1,), (1,)), ((), ())), preferred_element_type=F32)


def _mm_tn(a, b):
    return lax.dot_general(a.astype(MXU_DTYPE), b.astype(MXU_DTYPE),
                           (((0,), (0,)), ((), ())), preferred_element_type=F32)


def _split2(x):
    hi = x.astype(MXU_DTYPE)
    lo = (x - hi.astype(F32)).astype(MXU_DTYPE)
    return hi, lo


def _split3(x):
    hi = x.astype(MXU_DTYPE)
    r = x - hi.astype(F32)
    mid = r.astype(MXU_DTYPE)
    lo = (r - mid.astype(F32)).astype(MXU_DTYPE)
    return hi, mid, lo


def _mm_exact_rhs(c, x):
    hi, lo = _split2(x)
    return _mm(c, hi) + _mm(c, lo)


def _mm_exact_lhs3(x, c):
    hi, mid, lo = _split3(x)
    return _mm(hi, c) + _mm(mid, c) + _mm(lo, c)


def _sigmoid(x):
    return 1.0 / (1.0 + jnp.exp(-x))


def _softplus(x):
    return jnp.maximum(x, 0.0) + jnp.log(1.0 + jnp.exp(-jnp.abs(x)))


def _silu(x):
    return x * _sigmoid(x)


def _tile_rows(x, n):
    return jnp.concatenate([x] * n, axis=0)


def _block_diag_mask(rows, cols, rblk, cblk):
    r = lax.broadcasted_iota(jnp.int32, (rows, cols), 0) // rblk
    c = lax.broadcasted_iota(jnp.int32, (rows, cols), 1) // cblk
    return r == c


def _params(sem):
    return pltpu.CompilerParams(dimension_semantics=sem, vmem_limit_bytes=VMEM_LIMIT)


def _inproj_kernel(x_ref, nw_ref, w1_ref, w2_ref, w3_ref, o1_ref, o2_ref, o3_ref):
    x = x_ref[...]
    ms = jnp.mean(x * x, axis=-1, keepdims=True)
    h = (x * lax.rsqrt(ms + RMS_EPS) * nw_ref[...]).astype(MXU_DTYPE)
    o1_ref[...] = jnp.dot(h, w1_ref[...], preferred_element_type=F32)
    o2_ref[...] = jnp.dot(h, w2_ref[...], preferred_element_type=F32)
    o3_ref[...] = jnp.dot(h, w3_ref[...], preferred_element_type=F32)


def _inproj(xf, nw, w1, w2, w3, tm):
    t, d = xf.shape
    n1, n2, n3 = w1.shape[1], w2.shape[1], w3.shape[1]
    row = lambda i: (i, 0)
    fixed = lambda i: (0, 0)
    return pl.pallas_call(
        _inproj_kernel,
        grid=(t // tm,),
        in_specs=[pl.BlockSpec((tm, d), row), pl.BlockSpec((1, d), fixed),
                  pl.BlockSpec((d, n1), fixed), pl.BlockSpec((d, n2), fixed),
                  pl.BlockSpec((d, n3), fixed)],
        out_specs=[pl.BlockSpec((tm, n1), row), pl.BlockSpec((tm, n2), row),
                   pl.BlockSpec((tm, n3), row)],
        out_shape=[jax.ShapeDtypeStruct((t, n1), F32), jax.ShapeDtypeStruct((t, n2), F32),
                   jax.ShapeDtypeStruct((t, n3), F32)],
        compiler_params=_params(("arbitrary",)),
        name="inproj",
    )(xf, nw, w1, w2, w3)


def _gla_level_decays(g, cum, ridx):
    prev = pltpu.roll(g, 1, 0)
    nxt = pltpu.roll(g, CHUNK - 1, 0)
    r4 = ridx % 4
    logs = [jnp.where(ridx % 2 == 1, g, 0.0),
            jnp.where(r4 == 0, nxt, jnp.where(r4 == 1, 0.0, jnp.where(r4 == 2, g, g + prev)))]
    for lvl in range(2, CHUNK_LEVELS):
        half = 1 << lvl
        ref = jnp.concatenate(
            [jnp.broadcast_to(cum[b + half - 1:b + half, :], (2 * half, LANES))
             for b in range(0, CHUNK, 2 * half)], axis=0)
        logs.append(-jnp.abs(cum - ref))
    return [jnp.exp(d) for d in logs]


def _gla_kernel(q_ref, k_ref, v_ref, z_ref, a_ref, wg_ref, bg_ref, nw_ref, tri_ref,
                o_ref, st_ref, *, rows):
    @pl.when(pl.program_id(2) == 0)
    def _():
        st_ref[...] = jnp.zeros_like(st_ref)

    pre = _mm(a_ref[...], wg_ref[...]) + bg_ref[...]
    log_a = -_softplus(-pre) * (1.0 / GLA_GATE_TAU)

    lane = lax.broadcasted_iota(jnp.int32, (CHUNK, LANES), 1)
    head0 = lane < HEAD_DIM
    ridx = lax.broadcasted_iota(jnp.int32, (CHUNK, LANES), 0)
    head_ones = _block_diag_mask(LANES, LANES, HEAD_DIM, HEAD_DIM)
    head_ones_f = head_ones.astype(MXU_DTYPE)
    ri2 = lax.broadcasted_iota(jnp.int32, (2 * CHUNK, CHUNK), 0) % CHUNK
    ci2 = lax.broadcasted_iota(jnp.int32, (2 * CHUNK, CHUNK), 1)
    same_blk = [(ri2 >> (l + 1)) == (ci2 >> (l + 1)) for l in range(CHUNK_LEVELS)]
    upper = [((ridx >> l) & 1) == 1 for l in range(CHUNK_LEVELS)]
    tri = tri_ref[...]

    st = st_ref[...]
    outs = []
    for c in range(rows // CHUNK):
        sl = slice(c * CHUNK, (c + 1) * CHUNK)
        q = q_ref[sl, :] * (HEAD_DIM ** -0.5)
        k = k_ref[sl, :]
        v = v_ref[sl, :]
        g = log_a[sl, :]
        cum = _mm_exact_rhs(tri, g)
        last = cum[CHUNK - 1:CHUNK, :]
        q_dec = q * jnp.exp(cum)
        k_dec = k * jnp.exp(last - cum)
        e_last = jnp.exp(last)
        decays = _gla_level_decays(g, cum, ridx)

        a_st = jnp.zeros((2 * CHUNK, CHUNK), F32)
        for l in range(CHUNK_LEVELS):
            e = decays[l]
            ql = jnp.where(upper[l], q * e, 0.0)
            kl = jnp.where(upper[l], 0.0, k * e)
            lhs = jnp.concatenate([jnp.where(head0, ql, 0.0), jnp.where(head0, 0.0, ql)], axis=0)
            a_st = a_st + jnp.where(same_blk[l], _mm_nt(lhs, kl), 0.0)
        o_full = _mm(a_st, v)
        o_intra = jnp.where(head0, o_full[0:CHUNK], o_full[CHUNK:2 * CHUNK])
        diag = _mm(q * k, head_ones_f)
        outs.append(o_intra + diag * v + _mm_nt(q_dec, st))
        st = st * e_last + jnp.where(head_ones, _mm_tn(v, k_dec), 0.0)
    st_ref[...] = st

    o_all = jnp.concatenate(outs, axis=0)
    ms = _mm(o_all * o_all, head_ones_f) * (1.0 / HEAD_DIM)
    y = o_all * lax.rsqrt(ms + RMS_EPS) * nw_ref[...]
    o_ref[...] = y * _silu(z_ref[...])


def _gla(pg, wg, bg, nw, tri, batch, seq, rows):
    t = pg.shape[0]
    nt = seq // rows
    pairs = GLA_W // LANES
    col = lambda off: (lambda b, p, i: (b * nt + i, off + p))
    fixed = lambda b, p, i: (0, 0)
    blk = lambda off: pl.BlockSpec((rows, LANES), col(off))
    return pl.pallas_call(
        functools.partial(_gla_kernel, rows=rows),
        grid=(batch, pairs, nt),
        in_specs=[blk(0), blk(pairs), blk(2 * pairs), blk(3 * pairs),
                  pl.BlockSpec((rows, LANES), lambda b, p, i: (b * nt + i, 4 * pairs)),
                  pl.BlockSpec((LANES, LANES), lambda b, p, i: (0, p)),
                  pl.BlockSpec((1, LANES), lambda b, p, i: (0, p)),
                  pl.BlockSpec((1, LANES), fixed),
                  pl.BlockSpec(tri.shape, fixed)],
        out_specs=pl.BlockSpec((rows, LANES), lambda b, p, i: (b * nt + i, p)),
        out_shape=jax.ShapeDtypeStruct((t, GLA_W), F32),
        scratch_shapes=[pltpu.VMEM((LANES, LANES), F32)],
        compiler_params=_params(("arbitrary", "arbitrary", "arbitrary")),
        name="gla",
    )(pg, pg, pg, pg, pg, wg, bg, nw, tri)


def _gdn_kernel(q_ref, k_ref, v_ref, z_ref, e_ref, cq_ref, ck_ref, cv_ref, alog_ref, dt_ref,
                nw_ref, tri_ref, indb_ref, indg_ref, o_ref, bq_ref, bk_ref, bv_ref, st_ref,
                *, rows):
    w = GDN_W
    halo = 8

    @pl.when(pl.program_id(1) == 0)
    def _():
        st_ref[...] = jnp.zeros_like(st_ref)
        for buf in (bq_ref, bk_ref, bv_ref):
            buf[0:halo, :] = jnp.zeros((halo, w), F32)

    def conv_silu(x_ref, buf, cw_ref):
        buf[halo:halo + rows, :] = x_ref[...]
        full = buf[...]
        acc = cw_ref[CONV_WIDTH - 1:CONV_WIDTH, :] * full[halo:halo + rows]
        for back in range(1, CONV_WIDTH):
            tap = CONV_WIDTH - 1 - back
            acc = acc + cw_ref[tap:tap + 1, :] * pltpu.roll(full, back, 0)[halo:halo + rows]
        buf[0:halo, :] = buf[rows:rows + halo, :]
        return _silu(acc)

    q = conv_silu(q_ref, bq_ref, cq_ref)
    k = conv_silu(k_ref, bk_ref, ck_ref)
    v = conv_silu(v_ref, bv_ref, cv_ref)

    head_ones = _block_diag_mask(w, w, HEAD_DIM, HEAD_DIM)
    head_ones_f = head_ones.astype(MXU_DTYPE)

    sumsq = lambda x: _mm(x * x, head_ones_f)
    q = q * lax.rsqrt(sumsq(q) + RMS_EPS) * (HEAD_DIM ** -0.5)
    k = k * lax.rsqrt(sumsq(k) + RMS_EPS)

    extra = e_ref[...]
    beta_e = _mm_exact_lhs3(_sigmoid(extra), indb_ref[...])
    g_small = -jnp.exp(alog_ref[...]) * _softplus(extra + dt_ref[...])

    lane = lax.broadcasted_iota(jnp.int32, (CHUNK, w), 1) % HEAD_DIM
    ridx = lax.broadcasted_iota(jnp.int32, (CHUNK, w), 0)
    incl = lane <= ridx
    strict = lane < ridx
    tri = tri_ref[...]
    indg = indg_ref[...]

    bd = lambda x: jnp.where(head_ones, _tile_rows(x, GDN_HEADS), 0.0)
    eye_all = (lane == ridx).astype(F32)
    nchunk = rows // CHUNK

    xs, ps, keep = [], [], []
    for c in range(nchunk):
        sl = slice(c * CHUNK, (c + 1) * CHUNK)
        qc, kc, vc, bc = q[sl], k[sl], v[sl], beta_e[sl]
        h3 = _split3(g_small[sl])
        gc_small = _mm(tri, h3[0]) + _mm(tri, h3[1]) + _mm(tri, h3[2])
        gc = _mm_exact_lhs3(gc_small, indg)
        eg = jnp.exp(gc)
        g_last = gc[CHUNK - 1:CHUNK, :]
        kb = kc * bc

        g_hi, g_mid, g_lo = [p.astype(F32) for p in _split3(gc)]
        one = jnp.ones_like(gc)
        zero = jnp.zeros_like(gc)
        a2 = jnp.where(lane == 0, g_hi, jnp.where(lane == 1, g_mid, jnp.where(
            lane == 2, g_lo, jnp.where(lane < 6, one, zero))))
        b2 = jnp.where(lane < 3, one, jnp.where(lane == 3, -g_hi, jnp.where(
            lane == 4, -g_mid, jnp.where(lane == 5, -g_lo, zero))))
        decay = jnp.exp(jnp.where(incl, _mm_nt(a2, bd(b2)), NEG_BIG))

        kk = _mm_nt(jnp.concatenate([kb, qc], axis=0), bd(kc))
        xs.append(-jnp.where(strict, kk[0:CHUNK] * decay, 0.0))
        ps.append(eye_all)
        keep.append(dict(attn=kk[CHUNK:2 * CHUNK] * decay, vb=vc * bc, kbg=kb * eg, qg=qc * eg,
                         k_dec=kc * jnp.exp(g_last - gc), e_last=jnp.exp(g_last)))

    for lvl in range(CHUNK_LEVELS):
        for c in range(nchunk):
            rhs = bd(xs[c])
            if lvl < CHUNK_LEVELS - 1:
                r = _mm(jnp.concatenate([xs[c], ps[c]], axis=0), rhs)
                xs[c] = r[0:CHUNK]
                ps[c] = ps[c] + r[CHUNK:2 * CHUNK]
            else:
                ps[c] = ps[c] + _mm(ps[c], rhs)

    steps = []
    for c in range(nchunk):
        kp = keep[c]
        uw = _mm(ps[c], jnp.concatenate([bd(kp["vb"]), bd(kp["kbg"])], axis=1))
        u, wmat = uw[:, 0:w], uw[:, w:2 * w]
        ao = _mm(kp["attn"], jnp.concatenate([bd(wmat), bd(u)], axis=1))
        nb = _mm_tn(jnp.concatenate([wmat, u], axis=1), kp["k_dec"])
        steps.append((kp["qg"] - ao[:, 0:w], ao[:, w:2 * w], kp["e_last"],
                      jnp.where(head_ones, nb[0:w], 0.0), jnp.where(head_ones, nb[w:2 * w], 0.0)))

    st = st_ref[...]
    outs = []
    for q_eff, o_const, e_last, n_mat, b_mat in steps:
        outs.append(_mm_nt(q_eff, st) + o_const)
        st = st * e_last - _mm(st, n_mat) + b_mat
    st_ref[...] = st

    o_all = jnp.concatenate(outs, axis=0)
    ms = _mm(o_all * o_all, head_ones_f) * (1.0 / HEAD_DIM)
    y = o_all * lax.rsqrt(ms + RMS_EPS) * nw_ref[...]
    o_ref[...] = y * _silu(z_ref[...])


def _gdn(pd, cq, ck, cv, alog, dt, nw, tri, indb, indg, batch, seq, rows):
    t = pd.shape[0]
    nt = seq // rows
    w = GDN_W
    fixed = lambda b, i: (0, 0)
    blk = lambda off: pl.BlockSpec((rows, w), lambda b, i: (b * nt + i, off))
    small = lambda a: pl.BlockSpec(a.shape, fixed)
    return pl.pallas_call(
        functools.partial(_gdn_kernel, rows=rows),
        grid=(batch, nt),
        in_specs=[blk(0), blk(1), blk(2), blk(3),
                  pl.BlockSpec((rows, LANES), lambda b, i: (b * nt + i, 4 * w // LANES)),
                  small(cq), small(ck), small(cv), small(alog), small(dt), small(nw),
                  small(tri), small(indb), small(indg)],
        out_specs=pl.BlockSpec((rows, w), lambda b, i: (b * nt + i, 0)),
        out_shape=jax.ShapeDtypeStruct((t, w), F32),
        scratch_shapes=[pltpu.VMEM((rows + 8, w), F32), pltpu.VMEM((rows + 8, w), F32),
                        pltpu.VMEM((rows + 8, w), F32), pltpu.VMEM((w, w), F32)],
        compiler_params=_params(("arbitrary", "arbitrary")),
        name="gdn",
    )(pd, pd, pd, pd, pd, cq, ck, cv, alog, dt, nw, tri, indb, indg)


def _moba_prep_kernel(q_ref, k_ref, v_ref, cos_ref, sin_ref,
                      qo_ref, ko_ref, vt_ref, bias_ref, km_ref, *, nblk):
    n = pl.program_id(1)
    blk = MOBA_BLOCK
    hw = MOBA_W

    @pl.when(n == 0)
    def _():
        km_ref[...] = jnp.zeros_like(km_ref)

    cosf = cos_ref[...]
    sinf = sin_ref[...]
    lane = lax.broadcasted_iota(jnp.int32, (blk, LANES), 1)
    first_half = (lane % HEAD_DIM) < (HEAD_DIM // 2)

    def rope(x_ref):
        parts = []
        for c in range(hw // LANES):
            x = x_ref[:, c * LANES:(c + 1) * LANES]
            swapped = jnp.where(first_half, pltpu.roll(x, LANES - HEAD_DIM // 2, 1),
                                pltpu.roll(x, HEAD_DIM // 2, 1))
            parts.append(x * cosf + swapped * sinf)
        return jnp.concatenate(parts, axis=1)

    q = rope(q_ref) * (HEAD_DIM ** -0.5 * LOG2_E)
    k = rope(k_ref)

    km_rows = jnp.where(_block_diag_mask(MOBA_HEADS * nblk, hw, nblk, HEAD_DIM),
                        _tile_rows(km_ref[...], MOBA_HEADS), 0.0)
    km_hi, km_lo = _split2(km_rows)
    q_hi, q_lo = _split2(q)
    gate = _mm_nt(km_hi, q_hi) + _mm_nt(km_hi, q_lo) + _mm_nt(km_lo, q_hi)
    gate = gate.reshape(MOBA_HEADS, nblk, blk)
    kb = lax.broadcasted_iota(jnp.int32, (MOBA_HEADS, nblk, blk), 1)
    past = kb < n
    g = jnp.where(past, gate, -jnp.inf)
    sel = jnp.zeros((MOBA_HEADS, nblk, blk), jnp.bool_)
    for _ in range(MOBA_TOPK):
        m = jnp.max(g, axis=1, keepdims=True)
        first = jnp.min(jnp.where(g == m, kb, nblk), axis=1, keepdims=True)
        pick = kb == first
        sel = sel | (pick & past)
        g = jnp.where(pick, -jnp.inf, g)
    bias_ref[...] = jnp.where(sel, 0.0, NEG_BIG).reshape(MOBA_HEADS * nblk, blk)

    qo_ref[...] = q.astype(qo_ref.dtype)
    ko_ref[...] = k.astype(ko_ref.dtype)
    v_t = v_ref[...].T
    ones_row = (lax.broadcasted_iota(jnp.int32, (MOBA_VT_ROWS - HEAD_DIM, blk), 0) == 0).astype(F32)
    vt_ref[...] = jnp.concatenate(
        [part for h in range(MOBA_HEADS)
         for part in (v_t[h * HEAD_DIM:(h + 1) * HEAD_DIM], ones_row)], axis=0).astype(vt_ref.dtype)
    km_ref[pl.ds(n, 1), :] = jnp.mean(k, axis=0, keepdims=True)


def _moba_prep(pm, cos, sin, batch, seq):
    t = pm.shape[0]
    nblk = seq // MOBA_BLOCK
    hw = MOBA_W
    blk = lambda off: pl.BlockSpec((MOBA_BLOCK, hw), lambda b, i: (b * nblk + i, off))
    tab = pl.BlockSpec((MOBA_BLOCK, LANES), lambda b, i: (i, 0))
    row_out = pl.BlockSpec((MOBA_BLOCK, hw), lambda b, i: (b * nblk + i, 0))
    per_block = lambda r: pl.BlockSpec((None, None, r, MOBA_BLOCK), lambda b, i: (b, i, 0, 0))
    return pl.pallas_call(
        functools.partial(_moba_prep_kernel, nblk=nblk),
        grid=(batch, nblk),
        in_specs=[blk(0), blk(1), blk(2), tab, tab],
        out_specs=[row_out, row_out, per_block(MOBA_HEADS * MOBA_VT_ROWS),
                   per_block(MOBA_HEADS * nblk)],
        out_shape=[jax.ShapeDtypeStruct((t, hw), MXU_DTYPE),
                   jax.ShapeDtypeStruct((t, hw), MXU_DTYPE),
                   jax.ShapeDtypeStruct((batch, nblk, MOBA_HEADS * MOBA_VT_ROWS, MOBA_BLOCK),
                                        MXU_DTYPE),
                   jax.ShapeDtypeStruct((batch, nblk, MOBA_HEADS * nblk, MOBA_BLOCK), F32)],
        scratch_shapes=[pltpu.VMEM((nblk, hw), F32)],
        compiler_params=_params(("arbitrary", "arbitrary")),
        name="moba_prep",
    )(pm, pm, pm, cos, sin)


def _moba_attn_kernel(q_ref, k_ref, vt_ref, bias_ref, o_ref, sa_ref, sb_ref, *, nblk):
    m_step = pl.program_id(2)
    blk = MOBA_BLOCK
    two = 2 * blk
    heads = range(MOBA_STEP_HEADS)
    key = lax.broadcasted_iota(jnp.int32, (two, two), 0)
    qry = lax.broadcasted_iota(jnp.int32, (two, two), 1)
    lane = lax.broadcasted_iota(jnp.int32, (two, MOBA_STEP_HEADS * HEAD_DIM), 1)
    hv = lambda h: slice(h * MOBA_VT_ROWS, (h + 1) * MOBA_VT_ROWS)
    q_pair = q_ref[...]
    q_head = [jnp.where((lane // HEAD_DIM) == h, q_pair, jnp.zeros_like(q_pair)) for h in heads]
    zero_row = jnp.zeros((1, blk), F32)

    def produce(buf, pair):
        k_pair = k_ref[pl.ds(pl.multiple_of(pair * two, two), two), :]
        for h in heads:
            buf[h] = _mm_nt(k_pair, q_head[h])

    def bias_rows(h, item):
        if item is None:
            blk_a = 2 * m_step
            row_a = jnp.concatenate([zero_row, bias_ref[1, pl.ds(h * nblk + blk_a, 1), :]], axis=1)
            return blk_a, blk_a + 1, row_a, jnp.zeros((1, two), F32)
        blk_a = 2 * (item - 1)
        rows = [jnp.concatenate([bias_ref[0, pl.ds(h * nblk + j, 1), :],
                                 bias_ref[1, pl.ds(h * nblk + j, 1), :]], axis=1)
                for j in (blk_a, blk_a + 1)]
        return blk_a, blk_a + 1, rows[0], rows[1]

    def consume(buf, item, carry):
        new = []
        for h in heads:
            m, acc = carry[2 * h], carry[2 * h + 1]
            blk_a, blk_b, row_a, row_b = bias_rows(h, item)
            s = buf[h]
            if item is None:
                s = jnp.where(key <= qry, s, NEG_BIG)
            s_a, s_b = s[0:blk], s[blk:two]
            m_new = jnp.maximum(m, jnp.maximum(jnp.max(s_a, axis=0, keepdims=True) + row_a,
                                               jnp.max(s_b, axis=0, keepdims=True) + row_b))
            p = jnp.concatenate([jnp.exp2(s_a - (m_new - row_a)),
                                 jnp.exp2(s_b - (m_new - row_b))], axis=0)
            vt = jnp.concatenate([vt_ref[blk_a, hv(h), :], vt_ref[blk_b, hv(h), :]], axis=1)
            new += [m_new, jnp.exp2(m - m_new) * acc + _mm(vt, p)]
        return tuple(new)

    init = (jnp.full((1, two), NEG_BIG, F32), jnp.zeros((MOBA_VT_ROWS, two), F32)) * len(heads)
    produce(sa_ref, m_step)
    produce(sb_ref, 0)
    carry = consume(sa_ref, None, init)

    def body(t, carry):
        produce(sa_ref, 2 * t + 1)
        carry = consume(sb_ref, 2 * t + 1, carry)
        produce(sb_ref, 2 * t + 2)
        return consume(sa_ref, 2 * t + 2, carry)

    carry = lax.fori_loop(0, m_step // 2, body, carry)
    carry = lax.cond(m_step % 2 == 1, lambda c: consume(sb_ref, m_step, c), lambda c: c, carry)
    o_t = jnp.concatenate(
        [carry[2 * h + 1][0:HEAD_DIM] / carry[2 * h + 1][HEAD_DIM:HEAD_DIM + 1] for h in heads],
        axis=0)
    o_ref[...] = o_t.T


def _moba_attn(qr, kr, vt, bias, batch, seq):
    t = qr.shape[0]
    nblk = seq // MOBA_BLOCK
    nstep = nblk // 2
    nh = MOBA_STEP_HEADS
    groups = MOBA_HEADS // nh
    lanes = nh * HEAD_DIM
    two = 2 * MOBA_BLOCK
    return pl.pallas_call(
        functools.partial(_moba_attn_kernel, nblk=nblk),
        grid=(batch, groups, nstep),
        in_specs=[pl.BlockSpec((two, lanes), lambda b, p, i: (b * nstep + i, p)),
                  pl.BlockSpec((seq, lanes), lambda b, p, i: (b, p)),
                  pl.BlockSpec((None, nblk, nh * MOBA_VT_ROWS, MOBA_BLOCK),
                               lambda b, p, i: (b, 0, p, 0)),
                  pl.BlockSpec((None, 2, nh * nblk, MOBA_BLOCK), lambda b, p, i: (b, i, p, 0))],
        out_specs=pl.BlockSpec((two, lanes), lambda b, p, i: (b * nstep + i, p)),
        out_shape=jax.ShapeDtypeStruct((t, MOBA_W), F32),
        scratch_shapes=[pltpu.VMEM((nh, two, two), F32), pltpu.VMEM((nh, two, two), F32)],
        compiler_params=_params(("arbitrary", "arbitrary", "arbitrary")),
        name="moba_attn",
    )(qr, kr, vt, bias)


def _tail_kernel(og_ref, od_ref, om_ref, x_ref, w1_ref, w2_ref, w3_ref, nmix_ref, npre_ref,
                 wg_ref, wu_ref, wd_ref, npost_ref, o_ref, *, col_chunks):
    mix = (_mm(og_ref[...], w1_ref[...]) + _mm(od_ref[...], w2_ref[...])
           + _mm(om_ref[...], w3_ref[...]))
    ms = jnp.mean(mix * mix, axis=-1, keepdims=True)
    x1 = x_ref[...] + mix * lax.rsqrt(ms + RMS_EPS) * nmix_ref[...]

    ms = jnp.mean(x1 * x1, axis=-1, keepdims=True)
    h = (x1 * lax.rsqrt(ms + RMS_EPS) * npre_ref[...]).astype(MXU_DTYPE)
    y = None
    for lo, hi in col_chunks:
        g = jnp.dot(h, wg_ref[:, lo:hi], preferred_element_type=F32)
        u = jnp.dot(h, wu_ref[:, lo:hi], preferred_element_type=F32)
        part = _mm(_silu(g) * u, wd_ref[lo:hi, :])
        y = part if y is None else y + part
    ms = jnp.mean(y * y, axis=-1, keepdims=True)
    o_ref[...] = x1 + y * lax.rsqrt(ms + RMS_EPS) * npost_ref[...]


def _tail(og, od, om, xf, w1, w2, w3, nmix, npre, wg, wu, wd, npost, tm, chunk):
    t, d = xf.shape
    dff = wg.shape[1]
    col_chunks = tuple((lo, min(lo + chunk, dff)) for lo in range(0, dff, chunk))
    row = lambda i: (i, 0)
    fixed = lambda i: (0, 0)
    resident = lambda a: pl.BlockSpec(a.shape, fixed, pipeline_mode=pl.Buffered(1))
    tile = lambda a: pl.BlockSpec((tm, a.shape[1]), row)
    return pl.pallas_call(
        functools.partial(_tail_kernel, col_chunks=col_chunks),
        grid=(t // tm,),
        in_specs=[tile(og), tile(od), tile(om), tile(xf), resident(w1), resident(w2),
                  resident(w3), resident(nmix), resident(npre), resident(wg), resident(wu),
                  resident(wd), resident(npost)],
        out_specs=pl.BlockSpec((tm, d), row),
        out_shape=jax.ShapeDtypeStruct((t, d), F32),
        compiler_params=_params(("arbitrary",)),
        name="outproj_ffn",
    )(og, od, om, xf, w1, w2, w3, nmix, npre, wg, wu, wd, npost)


def _tile_sizes(tokens, seq):
    pick = lambda want, n: want if n % want == 0 else MOBA_BLOCK
    return dict(proj=pick(512, tokens), ffn=pick(512, tokens), gdn=pick(1024, seq),
                gla=pick(2048, seq))


def _rope_tables(seq):
    half = HEAD_DIM // 2
    inv = ROPE_THETA ** (-jnp.arange(half, dtype=F32) / half)
    ang = jnp.arange(seq).astype(F32)[:, None] * inv[None, :]
    cos, sin = jnp.cos(ang), jnp.sin(ang)
    reps = LANES // HEAD_DIM
    return (jnp.tile(jnp.concatenate([cos, cos], axis=1), (1, reps)),
            jnp.tile(jnp.concatenate([-sin, sin], axis=1), (1, reps)))


def _head_indicator(first_lane, heads):
    ind = np.zeros((LANES, heads * HEAD_DIM), np.float32)
    for h in range(heads):
        ind[first_lane + h, h * HEAD_DIM:(h + 1) * HEAD_DIM] = 1.0
    return jnp.asarray(ind, MXU_DTYPE)


def _lane_row(values, first_lane):
    return jnp.zeros((1, LANES), F32).at[0, first_lane:first_lane + values.shape[0]].set(
        values.astype(F32))


def kernel(x, norm_mix_pre, norm_mix_post, norm_ffn_pre, norm_ffn_post, w_in, w_o, gla_w_gate,
           gla_b_gate, gla_norm, gdn_conv, gdn_a_log, gdn_dt_bias, gdn_norm, ffn_w_gate, ffn_w_up,
           ffn_w_down):
    batch, seq, d_model = x.shape
    depth = w_in.shape[0]
    tokens = batch * seq
    assert seq % (2 * MOBA_BLOCK) == 0
    ts = _tile_sizes(tokens, seq)

    cos, sin = _rope_tables(seq)
    tri = jnp.asarray(np.tril(np.ones((CHUNK, CHUNK), np.float32)), MXU_DTYPE)
    indb = _head_indicator(0, GDN_HEADS)
    indg = _head_indicator(GDN_HEADS, GDN_HEADS)
    row2d = lambda a: a.reshape(1, -1).astype(F32)
    pad_cols = lambda a, n: jnp.pad(a, ((0, 0), (0, n - a.shape[1])))
    bf = lambda a: a.astype(MXU_DTYPE)

    o_ga = 4 * GLA_W
    o_d = o_ga + GLA_GATE_RANK
    o_db = o_d + 4 * GDN_W
    o_m = o_db + 2 * GDN_HEADS

    xf = x.reshape(tokens, d_model)
    for l in range(depth):
        wl = w_in[l]
        w_gla = bf(jnp.concatenate([wl[:, :o_ga], pad_cols(wl[:, o_ga:o_d], LANES)], axis=1))
        w_gdn = bf(jnp.concatenate([wl[:, o_d:o_db], pad_cols(wl[:, o_db:o_m], LANES)], axis=1))
        w_moba = bf(wl[:, o_m:])
        pg, pd, pm = _inproj(xf, row2d(norm_mix_pre[l]), w_gla, w_gdn, w_moba, ts["proj"])

        wgate = bf(jnp.pad(gla_w_gate[l], ((0, LANES - GLA_GATE_RANK), (0, 0))))
        o_gla = _gla(pg, wgate, row2d(gla_b_gate[l]),
                     row2d(jnp.tile(gla_norm[l], LANES // HEAD_DIM)), tri, batch, seq, ts["gla"])

        conv = gdn_conv[l].astype(F32)
        o_gdn = _gdn(pd, conv[:, :GDN_W], conv[:, GDN_W:2 * GDN_W], conv[:, 2 * GDN_W:],
                     _lane_row(gdn_a_log[l], GDN_HEADS), _lane_row(gdn_dt_bias[l], GDN_HEADS),
                     row2d(jnp.tile(gdn_norm[l], GDN_HEADS)), tri, indb, indg,
                     batch, seq, ts["gdn"])

        o_moba = _moba_attn(*_moba_prep(pm, cos, sin, batch, seq), batch, seq)

        wo = bf(w_o[l])
        xf = _tail(o_gla, o_gdn, o_moba, xf, wo[:GLA_W], wo[GLA_W:GLA_W + GDN_W],
                   wo[GLA_W + GDN_W:], row2d(norm_mix_post[l]), row2d(norm_ffn_pre[l]),
                   bf(ffn_w_gate[l]), bf(ffn_w_up[l]), bf(ffn_w_down[l]),
                   row2d(norm_ffn_post[l]), ts["ffn"], FFN_CHUNK)
    return xf.reshape(batch, seq, d_model)
```
